```python
import math
import jax, jax.numpy as jnp
from jax import lax
import numpy as np

D_MODEL = 2048
BATCH = 1
SEQ = 8192
DEPTH = 4
DEC_BATCH = 1
DEC_SEQ = 16384
PAST_LEN = 128

HEAD_DIM = 128
N_GROUPS_A = 8
N_HEADS_B = 8
WIDTH_A = N_GROUPS_A * HEAD_DIM
WIDTH_B = N_HEADS_B * HEAD_DIM
MIX_WIDTH = WIDTH_A + WIDTH_B
PROJ_WIDTH = 2 * WIDTH_A + 3 * WIDTH_B
CHUNK = 128
DILATED_PATTERNS = ((128, 1), (512, 4), (2048, 16))
ROT_DIM = HEAD_DIM // 4
ROPE_THETA = 500000.0
D_FF = ((8 * D_MODEL + 3 * 256 - 1) // (3 * 256)) * 256
EPS = 1e-6
NEG_INF = -1e30

kernel_name = 'hybrid_gmlp_dilated_attn_encoder'


def rms_norm(x, gain):
    x32 = x.astype(jnp.float32)
    y = x32 * lax.rsqrt(jnp.mean(x32 * x32, axis=-1, keepdims=True) + EPS)
    return (y * gain.astype(jnp.float32)).astype(x.dtype)


def partial_rope(x):
    S = x.shape[1]
    half = ROT_DIM // 2
    inv_freq = ROPE_THETA ** (-jnp.arange(half, dtype=jnp.float32) / half)
    ang = jnp.arange(S, dtype=jnp.float32)[:, None] * inv_freq[None, :]
    cos = jnp.cos(ang)[None, :, None, :]
    sin = jnp.sin(ang)[None, :, None, :]
    xr = x[..., :ROT_DIM].astype(jnp.float32)
    x1, x2 = xr[..., :half], xr[..., half:]
    rot = jnp.concatenate([x1 * cos - x2 * sin, x2 * cos + x1 * sin], axis=-1)
    return jnp.concatenate([rot.astype(x.dtype), x[..., ROT_DIM:]], axis=-1)


def chunked_spatial_gating(u, v, v_gain, ws, bias):
    B, S, _ = u.shape
    u = jax.nn.gelu(u.astype(jnp.float32))
    v = jax.nn.gelu(v.astype(jnp.float32)).reshape(B, S // CHUNK, CHUNK, N_GROUPS_A, HEAD_DIM)
    mu = jnp.mean(v, axis=-1, keepdims=True)
    var = jnp.mean(jnp.square(v - mu), axis=-1, keepdims=True)
    v = (v - mu) * lax.rsqrt(var + EPS) * v_gain.astype(jnp.float32).reshape(N_GROUPS_A, HEAD_DIM)
    mixed = jnp.einsum('gts,bnsgc->bntgc', ws.astype(jnp.float32), v) + bias.astype(jnp.float32).T[:, :, None]
    return (u * mixed.reshape(B, S, WIDTH_A)).astype(v_gain.dtype)


def dilated_window_attention(q, k, v, window, dilation):
    B, S, H, Dh = q.shape
    R = window // (2 * dilation)
    L = S // dilation
    pad = (-L) % R
    Lp = L + pad
    nb = Lp // R

    def to_sub(t):
        return t.reshape(B, L, dilation, H, Dh).astype(jnp.float32)

    qs = jnp.pad(to_sub(q), ((0, 0), (0, pad), (0, 0), (0, 0), (0, 0)))
    ks = jnp.pad(to_sub(k), ((0, 0), (R, pad + R), (0, 0), (0, 0), (0, 0)))
    vs = jnp.pad(to_sub(v), ((0, 0), (R, pad + R), (0, 0), (0, 0), (0, 0)))
    qb = qs.reshape(B, nb, R, dilation, H, Dh)

    def neighbours(t):
        tb = t.reshape(B, nb + 2, R, dilation, H, Dh)
        return jnp.concatenate([tb[:, :-2], tb[:, 1:-1], tb[:, 2:]], axis=2)

    kb = neighbours(ks)
    vb = neighbours(vs)
    scores = jnp.einsum('bnqrhd,bnkrhd->bnrhqk', qb, kb) * (Dh ** -0.5)
    qi = jnp.arange(nb)[:, None] * R + jnp.arange(R)[None, :]
    kj = jnp.arange(nb)[:, None] * R - R + jnp.arange(3 * R)[None, :]
    valid = (jnp.abs(qi[:, :, None] - kj[:, None, :]) <= R) & (kj[:, None, :] >= 0) & (kj[:, None, :] < L)
    scores = jnp.where(valid[None, :, None, None], scores, NEG_INF)
    m = jnp.max(scores, axis=-1, keepdims=True)
    p = jnp.exp(scores - m)
    l = jnp.sum(p, axis=-1)
    o = jnp.einsum('bnrhqk,bnkrhd->bnqrhd', p, vb)
    l_t = l.transpose(0, 1, 4, 2, 3)
    o = o / l_t[..., None]
    lse = m[..., 0].transpose(0, 1, 4, 2, 3) + jnp.log(l_t)
    o = o.reshape(B, Lp, dilation, H, Dh)[:, :L].reshape(B, S, H, Dh)
    lse = lse.reshape(B, Lp, dilation, H)[:, :L].reshape(B, S, H)
    return o, lse


def dilated_mixture_attention(q, k, v):
    outs = []
    lses = []
    for window, dilation in DILATED_PATTERNS:
        o, lse = dilated_window_attention(q, k, v, window, dilation)
        outs.append(o)
        lses.append(lse)
    w = jax.nn.softmax(jnp.stack(lses, axis=0), axis=0)
    return jnp.sum(w[..., None] * jnp.stack(outs, axis=0), axis=0)


def encoder_layer(x, g_mix_pre, w_in, gmlp_v_gain, gmlp_ws, gmlp_bias, out_gain_a, out_gain_b,
                  w_o, g_mix_post, g_ffn_pre, w_gate, w_up, w_down, g_ffn_post):
    B, S, _ = x.shape
    h = rms_norm(x, g_mix_pre)
    proj = h @ w_in
    u, va, q, k, vv = jnp.split(proj, [WIDTH_A, 2 * WIDTH_A, 2 * WIDTH_A + WIDTH_B, 2 * WIDTH_A + 2 * WIDTH_B], axis=-1)
    a_out = chunked_spatial_gating(u, va, gmlp_v_gain, gmlp_ws, gmlp_bias)
    q = partial_rope(q.reshape(B, S, N_HEADS_B, HEAD_DIM))
    k = partial_rope(k.reshape(B, S, N_HEADS_B, HEAD_DIM))
    vv = vv.reshape(B, S, N_HEADS_B, HEAD_DIM)
    b_out = dilated_mixture_attention(q, k, vv).reshape(B, S, WIDTH_B).astype(x.dtype)
    merged = jnp.concatenate([rms_norm(a_out, out_gain_a), rms_norm(b_out, out_gain_b)], axis=-1)
    x = x + rms_norm(merged @ w_o, g_mix_post)
    h = rms_norm(x, g_ffn_pre)
    f = (jax.nn.silu(h @ w_gate) * (h @ w_up)) @ w_down
    return x + rms_norm(f, g_ffn_post)


def trunk(x, g_mix_pre, w_in, gmlp_v_gain, gmlp_ws, gmlp_bias, out_gain_a, out_gain_b,
          w_o, g_mix_post, g_ffn_pre, w_gate, w_up, w_down, g_ffn_post):
    for l in range(DEPTH):
        x = encoder_layer(x, g_mix_pre[l], w_in[l], gmlp_v_gain[l], gmlp_ws[l], gmlp_bias[l],
                          out_gain_a[l], out_gain_b[l], w_o[l], g_mix_post[l], g_ffn_pre[l],
                          w_gate[l], w_up[l], w_down[l], g_ffn_post[l])
    return x


def setup_inputs(seed: int = 0) -> dict:
    key = jax.random.key(seed)
    ks = jax.random.split(key, 20)
    f32 = jnp.float32

    def nrm(k, shape, scale):
        return jax.random.normal(k, shape, f32) * scale

    def gain(k, shape):
        return 1.0 + 0.05 * jax.random.normal(k, shape, f32)

    return {
        'x_prompt': jax.random.normal(ks[0], (BATCH, SEQ, D_MODEL), f32),
        'x_sample': jax.random.normal(ks[1], (DEC_BATCH, DEC_SEQ, D_MODEL), f32),
        'g_mix_pre': gain(ks[2], (DEPTH, D_MODEL)),
        'w_in': nrm(ks[3], (DEPTH, D_MODEL, PROJ_WIDTH), D_MODEL ** -0.5),
        'gmlp_v_gain': gain(ks[4], (DEPTH, WIDTH_A)),
        'gmlp_ws': nrm(ks[5], (DEPTH, N_GROUPS_A, CHUNK, CHUNK), CHUNK ** -0.5),
        'gmlp_bias': nrm(ks[6], (DEPTH, N_GROUPS_A, CHUNK), 0.02),
        'out_gain_a': gain(ks[7], (DEPTH, WIDTH_A)),
        'out_gain_b': gain(ks[8], (DEPTH, WIDTH_B)),
        'w_o': nrm(ks[9], (DEPTH, MIX_WIDTH, D_MODEL), MIX_WIDTH ** -0.5),
        'g_mix_post': gain(ks[10], (DEPTH, D_MODEL)),
        'g_ffn_pre': gain(ks[11], (DEPTH, D_MODEL)),
        'w_gate': nrm(ks[12], (DEPTH, D_MODEL, D_FF), D_MODEL ** -0.5),
        'w_up': nrm(ks[13], (DEPTH, D_MODEL, D_FF), D_MODEL ** -0.5),
        'w_down': nrm(ks[14], (DEPTH, D_FF, D_MODEL), D_FF ** -0.5),
        'g_ffn_post': gain(ks[15], (DEPTH, D_MODEL)),
    }


def reference(x_prompt, x_sample, g_mix_pre, w_in, gmlp_v_gain, gmlp_ws, gmlp_bias, out_gain_a, out_gain_b,
              w_o, g_mix_post, g_ffn_pre, w_gate, w_up, w_down, g_ffn_post):
    y_prompt = trunk(x_prompt, g_mix_pre, w_in, gmlp_v_gain, gmlp_ws, gmlp_bias, out_gain_a, out_gain_b,
                     w_o, g_mix_post, g_ffn_pre, w_gate, w_up, w_down, g_ffn_post)
    y_sample = trunk(x_sample, g_mix_pre, w_in, gmlp_v_gain, gmlp_ws, gmlp_bias, out_gain_a, out_gain_b,
                     w_o, g_mix_post, g_ffn_pre, w_gate, w_up, w_down, g_ffn_post)
    return (y_prompt, y_sample)
```

```python
import functools

import jax
import jax.numpy as jnp
from jax import lax
from jax.experimental import pallas as pl
from jax.experimental.pallas import tpu as pltpu

D_MODEL = 2048
HEAD_DIM = 128
N_GROUPS = 8
WIDTH = N_GROUPS * HEAD_DIM
PROJ_WIDTH = 5 * WIDTH
CHUNK = 128
PATTERN_DILATIONS = (1, 4, 16)
RADIUS = 64
ROT_DIM = HEAD_DIM // 4
ROPE_THETA = 500000.0
D_FF = 5632
EPS = 1e-6
NEG_INF = -1e30

BF16 = jnp.bfloat16
F32 = jnp.float32

VMEM_LIMIT_BYTES = 56 * 1024 * 1024

TM_PROJ = 256
TM_FFN = 512
TF_FFN = 512
TQ_ATTN = 2048
HALO = RADIUS * PATTERN_DILATIONS[-1]
QBLK = 128
KBLK = QBLK + 2 * RADIUS


def _rms(x, gain):
    return x * lax.rsqrt(jnp.mean(x * x, axis=-1, keepdims=True) + EPS) * gain


def _dot(a, b):
    return jnp.dot(a, b, preferred_element_type=F32)


def _resident(shape):
    return pl.BlockSpec(shape, lambda *_: (0,) * len(shape), pipeline_mode=pl.Buffered(1))


def _params(n_grid_axes):
    return pltpu.CompilerParams(
        dimension_semantics=("arbitrary",) * n_grid_axes,
        vmem_limit_bytes=VMEM_LIMIT_BYTES,
    )


def _in_proj_kernel(x_ref, g_ref, w_ref, vgain_ref, cos_ref, sina_ref, sinb_ref,
                    gu_ref, vn_ref, q_ref, k_ref, v_ref):
    h = _rms(x_ref[...], g_ref[...]).astype(BF16)

    gu_ref[...] = jax.nn.gelu(_dot(h, w_ref[:, 0:WIDTH]))

    va = jax.nn.gelu(_dot(h, w_ref[:, WIDTH:2 * WIDTH]))
    for g in range(N_GROUPS):
        sl = slice(g * HEAD_DIM, (g + 1) * HEAD_DIM)
        vg = va[:, sl]
        mu = jnp.mean(vg, axis=-1, keepdims=True)
        cen = vg - mu
        var = jnp.mean(cen * cen, axis=-1, keepdims=True)
        vn_ref[:, sl] = (cen * lax.rsqrt(var + EPS) * vgain_ref[:, sl]).astype(BF16)

    cos = cos_ref[...]
    sina = sina_ref[...]
    sinb = sinb_ref[...]

    def rope_store(dst_ref, col0):
        y = _dot(h, w_ref[:, col0:col0 + WIDTH])
        for hd in range(N_GROUPS):
            sl = slice(hd * HEAD_DIM, (hd + 1) * HEAD_DIM)
            yh = y[:, sl]
            dst_ref[:, sl] = (yh * cos
                              + pltpu.roll(yh, ROT_DIM // 2, axis=1) * sina
                              + pltpu.roll(yh, HEAD_DIM - ROT_DIM // 2, axis=1) * sinb)

    rope_store(q_ref, 2 * WIDTH)
    rope_store(k_ref, 3 * WIDTH)
    v_ref[...] = _dot(h, w_ref[:, 4 * WIDTH:5 * WIDTH])


def _in_proj(x, g, w, vgain, cos_t, sina_t, sinb_t):
    s = x.shape[0]
    tm = TM_PROJ
    row = lambda width: pl.BlockSpec((tm, width), lambda i: (i, 0))
    f32_out = jax.ShapeDtypeStruct((s, WIDTH), F32)
    return pl.pallas_call(
        _in_proj_kernel,
        grid=(s // tm,),
        in_specs=[row(D_MODEL), _resident((1, D_MODEL)), _resident((D_MODEL, PROJ_WIDTH)),
                  _resident((1, WIDTH)), row(HEAD_DIM), row(HEAD_DIM), row(HEAD_DIM)],
        out_specs=[row(WIDTH)] * 5,
        out_shape=[f32_out, jax.ShapeDtypeStruct((s, WIDTH), BF16), f32_out, f32_out, f32_out],
        compiler_params=_params(1),
        name="in_proj",
    )(x, g, w, vgain, cos_t, sina_t, sinb_t)


def _gmlp_kernel(gu_ref, vn_ref, ws_ref, bias_ref, gain_ref, o_ref):
    tm = gu_ref.shape[0]
    for pair in range(tm // (2 * CHUNK)):
        r0 = pair * 2 * CHUNK
        rows = (slice(r0, r0 + CHUNK), slice(r0 + CHUNK, r0 + 2 * CHUNK))
        gated = ([], [])
        for g in range(N_GROUPS):
            sl = slice(g * HEAD_DIM, (g + 1) * HEAD_DIM)
            vpair = jnp.concatenate([vn_ref[rows[0], sl], vn_ref[rows[1], sl]], axis=1)
            mixed = _dot(ws_ref[g], vpair) + bias_ref[g]
            gated[0].append(gu_ref[rows[0], sl] * mixed[:, :HEAD_DIM])
            gated[1].append(gu_ref[rows[1], sl] * mixed[:, HEAD_DIM:])
        for c in range(2):
            a = jnp.concatenate(gated[c], axis=1)
            o_ref[rows[c], :] = _rms(a, gain_ref[...]).astype(BF16)


def _gmlp(gu, vn, ws, bias, gain):
    s = gu.shape[0]
    tm = TM_PROJ
    row = pl.BlockSpec((tm, WIDTH), lambda i: (i, 0))
    return pl.pallas_call(
        _gmlp_kernel,
        grid=(s // tm,),
        in_specs=[row, row, _resident((N_GROUPS, CHUNK, CHUNK)), _resident((N_GROUPS, CHUNK, 1)),
                  _resident((1, WIDTH))],
        out_specs=row,
        out_shape=jax.ShapeDtypeStruct((s, WIDTH), BF16),
        compiler_params=_params(1),
        name="gmlp",
    )(gu, vn, ws, bias, gain)


def _attn_kernel(q_ref, kp_ref, kc_ref, kn_ref, vp_ref, vc_ref, vn_ref, o_ref,
                 kwin, vwin, band, acc_s, m_s, l_s, *, seq_len):
    tile = pl.program_id(0)

    kwin[0:HALO, :] = kp_ref[...]
    kwin[HALO:HALO + TQ_ATTN, :] = kc_ref[...]
    kwin[HALO + TQ_ATTN:, :] = kn_ref[...]
    vwin[0:HALO, :] = vp_ref[...]
    vwin[HALO:HALO + TQ_ATTN, :] = vc_ref[...]
    vwin[HALO + TQ_ATTN:, :] = vn_ref[...]

    qa = lax.broadcasted_iota(jnp.int32, (QBLK, KBLK), 0)
    kb = lax.broadcasted_iota(jnp.int32, (QBLK, KBLK), 1)
    band[...] = jnp.where((kb >= qa) & (kb <= qa + 2 * RADIUS), 0.0, NEG_INF).astype(F32)

    scale = HEAD_DIM ** -0.5
    ones_cols = jnp.ones((KBLK, HEAD_DIM), BF16)
    key_lane = lax.broadcasted_iota(jnp.int32, (1, KBLK), 1)

    for pi, d in enumerate(PATTERN_DILATIONS):
        per_res = TQ_ATTN // d
        sub_len = seq_len // d
        sub_tile0 = tile * per_res
        n_blocks = TQ_ATTN // QBLK

        def block(idx, carry, d=d, pi=pi, per_res=per_res, sub_len=sub_len, sub_tile0=sub_tile0):
            r = lax.rem(idx, d)
            l0 = lax.div(idx, d) * QBLK
            q_row = l0 * d + r
            k_row = HALO + (l0 - RADIUS) * d + r
            q = q_ref[pl.ds(q_row, QBLK, stride=d), :].astype(BF16)
            k = kwin[pl.ds(k_row, KBLK, stride=d), :].astype(BF16)
            v = vwin[pl.ds(k_row, KBLK, stride=d), :].astype(BF16)
            s = lax.dot_general(q, k, (((1,), (1,)), ((), ())), preferred_element_type=F32) * scale
            key_sub = sub_tile0 + l0 - RADIUS + key_lane
            edge = jnp.where((key_sub >= 0) & (key_sub < sub_len), 0.0, NEG_INF).astype(F32)
            s = s + band[...] + edge
            m = jnp.max(s, axis=-1, keepdims=True)
            p = jnp.exp(s - m).astype(BF16)
            pv = _dot(p, jnp.concatenate([v, ones_cols], axis=1))
            rows = pl.ds(q_row, QBLK, stride=d)
            acc_s[pi, rows, :] = pv[:, :HEAD_DIM]
            l_s[pi, rows, :] = pv[:, HEAD_DIM:]
            m_s[pi, rows, :] = jnp.broadcast_to(m, (QBLK, HEAD_DIM))
            return carry

        lax.fori_loop(0, n_blocks, block, 0)

    def merge(c, carry):
        rows = pl.ds(pl.multiple_of(c * QBLK, QBLK), QBLK)
        ms = [m_s[pi, rows, :] for pi in range(len(PATTERN_DILATIONS))]
        top = jnp.maximum(jnp.maximum(ms[0], ms[1]), ms[2])
        num = jnp.zeros((QBLK, HEAD_DIM), F32)
        den = jnp.zeros((QBLK, HEAD_DIM), F32)
        for pi in range(len(PATTERN_DILATIONS)):
            e = jnp.exp(ms[pi] - top)
            num = num + e * acc_s[pi, rows, :]
            den = den + e * l_s[pi, rows, :]
        o_ref[rows, :] = num / den
        return carry

    lax.fori_loop(0, TQ_ATTN // QBLK, merge, 0)


def _attn(q, k, v):
    s = q.shape[0]
    n_tiles = s // TQ_ATTN
    halo_per_tile = TQ_ATTN // HALO
    n_halo_blocks = s // HALO
    cur = pl.BlockSpec((TQ_ATTN, HEAD_DIM), lambda t, h: (t, h))
    prev = pl.BlockSpec((HALO, HEAD_DIM), lambda t, h: (jnp.maximum(t * halo_per_tile - 1, 0), h))
    nxt = pl.BlockSpec((HALO, HEAD_DIM),
                       lambda t, h: (jnp.minimum((t + 1) * halo_per_tile, n_halo_blocks - 1), h))
    win = pltpu.VMEM((TQ_ATTN + 2 * HALO, HEAD_DIM), F32)
    per_pattern = pltpu.VMEM((len(PATTERN_DILATIONS), TQ_ATTN, HEAD_DIM), F32)
    return pl.pallas_call(
        functools.partial(_attn_kernel, seq_len=s),
        grid=(n_tiles, N_GROUPS),
        in_specs=[cur, prev, cur, nxt, prev, cur, nxt],
        out_specs=cur,
        out_shape=jax.ShapeDtypeStruct((s, WIDTH), F32),
        scratch_shapes=[win, win, pltpu.VMEM((QBLK, KBLK), F32), per_pattern, per_pattern, per_pattern],
        compiler_params=_params(2),
        name="attn",
    )(q, k, k, k, v, v, v)


def _out_proj_kernel(ma_ref, bo_ref, gb_ref, w_ref, gpost_ref, x_ref, gffn_ref, x1_ref, h2_ref):
    mb = _rms(bo_ref[...], gb_ref[...]).astype(BF16)
    mix = _dot(jnp.concatenate([ma_ref[...], mb], axis=1), w_ref[...])
    x1 = x_ref[...] + _rms(mix, gpost_ref[...])
    x1_ref[...] = x1
    h2_ref[...] = _rms(x1, gffn_ref[...]).astype(BF16)


def _out_proj(ma, bo, gb, w, gpost, x, gffn):
    s = x.shape[0]
    tm = TM_PROJ
    row = lambda width: pl.BlockSpec((tm, width), lambda i: (i, 0))
    return pl.pallas_call(
        _out_proj_kernel,
        grid=(s // tm,),
        in_specs=[row(WIDTH), row(WIDTH), _resident((1, WIDTH)), _resident((2 * WIDTH, D_MODEL)),
                  _resident((1, D_MODEL)), row(D_MODEL), _resident((1, D_MODEL))],
        out_specs=[row(D_MODEL), row(D_MODEL)],
        out_shape=[jax.ShapeDtypeStruct((s, D_MODEL), F32), jax.ShapeDtypeStruct((s, D_MODEL), BF16)],
        compiler_params=_params(1),
        name="out_proj",
    )(ma, bo, gb, w, gpost, x, gffn)


def _ffn_kernel(h_ref, wg_ref, wu_ref, wd_ref, x1_ref, gpost_ref, o_ref):
    f = pl.program_id(1)
    h = h_ref[...]
    act = (jax.nn.silu(_dot(h, wg_ref[...])) * _dot(h, wu_ref[...])).astype(BF16)
    part = _dot(act, wd_ref[...])

    @pl.when(f == 0)
    def _():
        o_ref[...] = part

    @pl.when(f > 0)
    def _():
        o_ref[...] += part

    @pl.when(f == pl.num_programs(1) - 1)
    def _():
        o_ref[...] = x1_ref[...] + _rms(o_ref[...], gpost_ref[...])


def _ffn(h2, wg, wu, wd, x1, gpost):
    s = x1.shape[0]
    tm, tf = TM_FFN, TF_FFN
    row = pl.BlockSpec((tm, D_MODEL), lambda i, f: (i, 0))
    return pl.pallas_call(
        _ffn_kernel,
        grid=(s // tm, D_FF // tf),
        in_specs=[row, pl.BlockSpec((D_MODEL, tf), lambda i, f: (0, f)),
                  pl.BlockSpec((D_MODEL, tf), lambda i, f: (0, f)),
                  pl.BlockSpec((tf, D_MODEL), lambda i, f: (f, 0)), row, _resident((1, D_MODEL))],
        out_specs=row,
        out_shape=jax.ShapeDtypeStruct((s, D_MODEL), F32),
        compiler_params=_params(2),
        name="ffn",
    )(h2, wg, wu, wd, x1, gpost)


def _rope_tables(s):
    half = ROT_DIM // 2
    inv_freq = ROPE_THETA ** (-jnp.arange(half, dtype=F32) / half)
    ang = jnp.arange(s, dtype=F32)[:, None] * inv_freq[None, :]
    cos, sin = jnp.cos(ang), jnp.sin(ang)
    zeros = jnp.zeros_like(sin)
    rest = HEAD_DIM - ROT_DIM
    cos_t = jnp.concatenate([cos, cos, jnp.ones((s, rest), F32)], axis=1)
    sina_t = jnp.concatenate([zeros, sin, jnp.zeros((s, rest), F32)], axis=1)
    sinb_t = jnp.concatenate([-sin, zeros, jnp.zeros((s, rest), F32)], axis=1)
    return cos_t, sina_t, sinb_t


def _trunk(x, layers):
    s = x.shape[0]
    tables = _rope_tables(s)
    for p in layers:
        gu, vn, q, k, v = _in_proj(x, p["g_mix_pre"], p["w_in"], p["gmlp_v_gain"], *tables)
        ma = _gmlp(gu, vn, p["gmlp_ws"], p["gmlp_bias"], p["out_gain_a"])
        bo = _attn(q, k, v)
        x1, h2 = _out_proj(ma, bo, p["out_gain_b"], p["w_o"], p["g_mix_post"], x, p["g_ffn_pre"])
        x = _ffn(h2, p["w_gate"], p["w_up"], p["w_down"], x1, p["g_ffn_post"])
    return x


def kernel(x_prompt, x_sample, g_mix_pre, w_in, gmlp_v_gain, gmlp_ws, gmlp_bias, out_gain_a, out_gain_b,
           w_o, g_mix_post, g_ffn_pre, w_gate, w_up, w_down, g_ffn_post):
    depth = w_in.shape[0]
    layers = []
    for l in range(depth):
        layers.append({
            "g_mix_pre": g_mix_pre[l][None, :],
            "w_in": w_in[l].astype(BF16),
            "gmlp_v_gain": gmlp_v_gain[l][None, :],
            "gmlp_ws": gmlp_ws[l].astype(BF16),
            "gmlp_bias": gmlp_bias[l][:, :, None],
            "out_gain_a": out_gain_a[l][None, :],
            "out_gain_b": out_gain_b[l][None, :],
            "w_o": w_o[l].astype(BF16),
            "g_mix_post": g_mix_post[l][None, :],
            "g_ffn_pre": g_ffn_pre[l][None, :],
            "w_gate": w_gate[l].astype(BF16),
            "w_up": w_up[l].astype(BF16),
            "w_down": w_down[l].astype(BF16),
            "g_ffn_post": g_ffn_post[l][None, :],
        })
    outs = []
    for x in (x_prompt, x_sample):
        b, s, dm = x.shape
        if b == 1:
            outs.append(_trunk(x.reshape(s, dm), layers).reshape(1, s, dm))
        else:
            outs.append(jnp.stack([_trunk(x[i], layers) for i in range(b)]))
    return tuple(outs)
```

```python
import functools

import jax
import jax.numpy as jnp
from jax import lax
from jax.experimental import pallas as pl
from jax.experimental.pallas import tpu as pltpu

D_MODEL = 2048
HEAD_DIM = 128
N_GROUPS = 8
WIDTH = N_GROUPS * HEAD_DIM
PROJ_WIDTH = 5 * WIDTH
CHUNK = 128
PATTERN_DILATIONS = (1, 4, 16)
RADIUS = 64
ROT_DIM = HEAD_DIM // 4
ROPE_THETA = 500000.0
D_FF = 5632
EPS = 1e-6
NEG_INF = -1e30
LOG2E = 1.4426950408889634

BF16 = jnp.bfloat16
F32 = jnp.float32

VMEM_LIMIT_BYTES = 56 * 1024 * 1024

TM_PROJ = 256
TM_SUB = 128
TM_FFN = 512
TF_FFN = 512
TQ_ATTN = 2048
HALO = RADIUS * PATTERN_DILATIONS[-1]
QBLK = 128
KBLK = QBLK + 2 * RADIUS
N_RES4 = 4
SUB4 = TQ_ATTN // N_RES4
HALO4 = HALO // N_RES4
BLOCKS_PER_RES4 = SUB4 // QBLK


def _rms(x, gain):
    return x * lax.rsqrt(jnp.mean(x * x, axis=-1, keepdims=True) + EPS) * gain


def _dot(a, b):
    return jnp.dot(a, b, preferred_element_type=F32)


def _resident(shape, layer):
    return pl.BlockSpec((None,) + shape, lambda *_: (layer,) + (0,) * len(shape),
                        pipeline_mode=pl.Buffered(1))


def _params(n_grid_axes):
    return pltpu.CompilerParams(
        dimension_semantics=("arbitrary",) * n_grid_axes,
        vmem_limit_bytes=VMEM_LIMIT_BYTES,
    )


def _spatial_gating(gu_ref, vn_ref, ws_ref, bias_ref, gain_ref, o_ref):
    for pair in range(gu_ref.shape[0] // (2 * CHUNK)):
        r0 = pair * 2 * CHUNK
        rows = (slice(r0, r0 + CHUNK), slice(r0 + CHUNK, r0 + 2 * CHUNK))
        gated = ([], [])
        for g in range(N_GROUPS):
            sl = slice(g * HEAD_DIM, (g + 1) * HEAD_DIM)
            vpair = jnp.concatenate([vn_ref[rows[0], sl], vn_ref[rows[1], sl]], axis=1)
            mixed = _dot(ws_ref[g], vpair) + bias_ref[g]
            gated[0].append(gu_ref[rows[0], sl] * mixed[:, :HEAD_DIM])
            gated[1].append(gu_ref[rows[1], sl] * mixed[:, HEAD_DIM:])
        for c in range(2):
            a = jnp.concatenate(gated[c], axis=1)
            o_ref[rows[c], :] = _rms(a, gain_ref[...]).astype(BF16)


def _in_proj_kernel(x_ref, g_ref, w_ref, vgain_ref, ws_ref, bias_ref, again_ref,
                    cos_ref, sina_ref, sinb_ref, ma_ref, q_ref, k_ref, v_ref, gu_s, vn_s):
    h = _rms(x_ref[...], g_ref[...]).astype(BF16)

    gu_s[...] = jax.nn.gelu(_dot(h, w_ref[:, 0:WIDTH]))

    va = jax.nn.gelu(_dot(h, w_ref[:, WIDTH:2 * WIDTH]))
    for g in range(N_GROUPS):
        sl = slice(g * HEAD_DIM, (g + 1) * HEAD_DIM)
        vg = va[:, sl]
        mu = jnp.mean(vg, axis=-1, keepdims=True)
        cen = vg - mu
        var = jnp.mean(cen * cen, axis=-1, keepdims=True)
        vn_s[:, sl] = (cen * lax.rsqrt(var + EPS) * vgain_ref[:, sl]).astype(BF16)

    _spatial_gating(gu_s, vn_s, ws_ref, bias_ref, again_ref, ma_ref)

    cos = cos_ref[...]
    sina = sina_ref[...]
    sinb = sinb_ref[...]

    def rope_store(dst_ref, col0):
        y = _dot(h, w_ref[:, col0:col0 + WIDTH])
        for hd in range(N_GROUPS):
            sl = slice(hd * HEAD_DIM, (hd + 1) * HEAD_DIM)
            yh = y[:, sl]
            dst_ref[:, sl] = (yh * cos
                              + pltpu.roll(yh, ROT_DIM // 2, axis=1) * sina
                              + pltpu.roll(yh, HEAD_DIM - ROT_DIM // 2, axis=1) * sinb)

    rope_store(q_ref, 2 * WIDTH)
    rope_store(k_ref, 3 * WIDTH)
    v_ref[...] = _dot(h, w_ref[:, 4 * WIDTH:5 * WIDTH])


def _in_proj(layer, x, g, w, vgain, ws, bias, again, cos_t, sina_t, sinb_t):
    s = x.shape[0]
    tm = TM_PROJ
    row = lambda width: pl.BlockSpec((tm, width), lambda i: (i, 0))
    f32_out = jax.ShapeDtypeStruct((s, WIDTH), F32)
    return pl.pallas_call(
        _in_proj_kernel,
        grid=(s // tm,),
        in_specs=[row(D_MODEL), _resident((1, D_MODEL), layer), _resident((D_MODEL, PROJ_WIDTH), layer),
                  _resident((1, WIDTH), layer), _resident((N_GROUPS, CHUNK, CHUNK), layer),
                  _resident((N_GROUPS, CHUNK, 1), layer), _resident((1, WIDTH), layer),
                  row(HEAD_DIM), row(HEAD_DIM), row(HEAD_DIM)],
        out_specs=[row(WIDTH)] * 4,
        out_shape=[jax.ShapeDtypeStruct((s, WIDTH), BF16), f32_out, f32_out, f32_out],
        scratch_shapes=[pltpu.VMEM((tm, WIDTH), F32), pltpu.VMEM((tm, WIDTH), BF16)],
        compiler_params=_params(1),
        name="in_proj",
    )(x, g, w, vgain, ws, bias, again, cos_t, sina_t, sinb_t)


def _attn_kernel(q_ref, kp_ref, kc_ref, kn_ref, vp_ref, vc_ref, vn_ref, o_ref,
                 knat, vnat, q4, k4, v4, band, acc_s, m_s, l_s, *, seq_len):
    tile = pl.program_id(0)
    kv_refs = ((knat, k4, kp_ref, kc_ref, kn_ref), (vnat, v4, vp_ref, vc_ref, vn_ref))

    for nat, _, p_ref, c_ref, n_ref in kv_refs:
        nat[0:RADIUS, :] = p_ref[HALO - RADIUS:HALO, :]
        nat[RADIUS:RADIUS + TQ_ATTN, :] = c_ref[...]
        nat[RADIUS + TQ_ATTN:, :] = n_ref[0:RADIUS, :]
    for r in range(N_RES4):
        q4[r] = q_ref[pl.ds(r, SUB4, stride=N_RES4), :]
        for _, win4, p_ref, c_ref, n_ref in kv_refs:
            win4[r, 0:HALO4, :] = p_ref[pl.ds(r, HALO4, stride=N_RES4), :]
            win4[r, HALO4:HALO4 + SUB4, :] = c_ref[pl.ds(r, SUB4, stride=N_RES4), :]
            win4[r, HALO4 + SUB4:, :] = n_ref[pl.ds(r, HALO4, stride=N_RES4), :]

    qa = lax.broadcasted_iota(jnp.int32, (QBLK, KBLK), 0)
    kb = lax.broadcasted_iota(jnp.int32, (QBLK, KBLK), 1)
    band[...] = jnp.where((kb >= qa) & (kb <= qa + 2 * RADIUS), 0.0, NEG_INF).astype(F32)

    log2_scale = HEAD_DIM ** -0.5 * LOG2E
    ones_cols = jnp.ones((KBLK, HEAD_DIM), BF16)
    key_lane = lax.broadcasted_iota(jnp.int32, (1, KBLK), 1)

    def score_block(q, k, v, first_key_sub, sub_len):
        s = lax.dot_general(q.astype(BF16), k.astype(BF16), (((1,), (1,)), ((), ())),
                            preferred_element_type=F32) * log2_scale
        key_sub = first_key_sub + key_lane
        edge = jnp.where((key_sub >= 0) & (key_sub < sub_len), 0.0, NEG_INF).astype(F32)
        s = s + band[...] + edge
        m = jnp.max(s, axis=-1, keepdims=True)
        p = jnp.exp2(s - m).astype(BF16)
        pv = _dot(p, jnp.concatenate([v.astype(BF16), ones_cols], axis=1))
        return pv[:, :HEAD_DIM], pv[:, HEAD_DIM:], jnp.broadcast_to(m, (QBLK, HEAD_DIM))

    def keep(pattern, idx, result):
        acc_s[(pattern,) + idx] = result[0]
        l_s[(pattern,) + idx] = result[1]
        m_s[(pattern,) + idx] = result[2]

    def dilation1(g):
        for u in range(BLOCKS_PER_RES4):
            row0 = g * SUB4 + u * QBLK
            keys = pl.ds(row0, KBLK)
            res = score_block(q_ref[pl.ds(row0, QBLK), :], knat[keys, :], vnat[keys, :],
                              tile * TQ_ATTN + row0 - RADIUS, seq_len)
            keep(0, (g, slice(u * QBLK, (u + 1) * QBLK), slice(None)), res)

    def dilation4(r):
        for jb in range(BLOCKS_PER_RES4):
            l0 = jb * QBLK
            keys = slice(HALO4 - RADIUS + l0, HALO4 - RADIUS + l0 + KBLK)
            res = score_block(q4[r, l0:l0 + QBLK, :], k4[r, keys, :], v4[r, keys, :],
                              tile * SUB4 + l0 - RADIUS, seq_len // 4)
            keep(1, (r, slice(l0, l0 + QBLK), slice(None)), res)

    def dilation16(r):
        for jj in range(N_RES4):
            keys = pl.ds(jj, KBLK, stride=N_RES4)
            queries = pl.ds(jj, QBLK, stride=N_RES4)
            res = score_block(q4[r, queries, :], k4[r, keys, :], v4[r, keys, :],
                              tile * (TQ_ATTN // 16) - RADIUS, seq_len // 16)
            keep(2, (r, queries, slice(None)), res)

    def merge(r):
        for c in range(BLOCKS_PER_RES4):
            perm = (r, slice(c * QBLK, (c + 1) * QBLK), slice(None))
            nat = (c, pl.ds(r, QBLK, stride=N_RES4), slice(None))
            idx = ((0,) + nat, (1,) + perm, (2,) + perm)
            ms = [m_s[i] for i in idx]
            top = jnp.maximum(jnp.maximum(ms[0], ms[1]), ms[2])
            num = jnp.zeros((QBLK, HEAD_DIM), F32)
            den = jnp.zeros((QBLK, HEAD_DIM), F32)
            for i, m in zip(idx, ms):
                e = jnp.exp2(m - top)
                num = num + e * acc_s[i]
                den = den + e * l_s[i]
            o_ref[pl.ds(c * SUB4 + r, QBLK, stride=N_RES4), :] = num / den

    for phase in (dilation1, dilation4, dilation16, merge):
        for r in range(N_RES4):
            phase(r)


def _attn(q, k, v):
    s = q.shape[0]
    n_tiles = s // TQ_ATTN
    halo_per_tile = TQ_ATTN // HALO
    n_halo_blocks = s // HALO
    cur = pl.BlockSpec((TQ_ATTN, HEAD_DIM), lambda t, h: (t, h))
    prev = pl.BlockSpec((HALO, HEAD_DIM), lambda t, h: (jnp.maximum(t * halo_per_tile - 1, 0), h))
    nxt = pl.BlockSpec((HALO, HEAD_DIM),
                       lambda t, h: (jnp.minimum((t + 1) * halo_per_tile, n_halo_blocks - 1), h))
    nat_win = pltpu.VMEM((TQ_ATTN + 2 * RADIUS, HEAD_DIM), F32)
    win4 = pltpu.VMEM((N_RES4, SUB4 + 2 * HALO4, HEAD_DIM), F32)
    per_pattern = pltpu.VMEM((len(PATTERN_DILATIONS), N_RES4, SUB4, HEAD_DIM), F32)
    return pl.pallas_call(
        functools.partial(_attn_kernel, seq_len=s),
        grid=(n_tiles, N_GROUPS),
        in_specs=[cur, prev, cur, nxt, prev, cur, nxt],
        out_specs=cur,
        out_shape=jax.ShapeDtypeStruct((s, WIDTH), F32),
        scratch_shapes=[nat_win, nat_win, pltpu.VMEM((N_RES4, SUB4, HEAD_DIM), F32), win4, win4,
                        pltpu.VMEM((QBLK, KBLK), F32), per_pattern, per_pattern, per_pattern],
        compiler_params=_params(2),
        name="attn",
    )(q, k, k, k, v, v, v)


def _out_proj_kernel(ma_ref, bo_ref, gb_ref, w_ref, gpost_ref, x_ref, gffn_ref, x1_ref, h2_ref):
    for r0 in range(0, ma_ref.shape[0], TM_SUB):
        rows = slice(r0, r0 + TM_SUB)
        mb = _rms(bo_ref[rows, :], gb_ref[...]).astype(BF16)
        mix = _dot(jnp.concatenate([ma_ref[rows, :], mb], axis=1), w_ref[...])
        x1 = x_ref[rows, :] + _rms(mix, gpost_ref[...])
        x1_ref[rows, :] = x1
        h2_ref[rows, :] = _rms(x1, gffn_ref[...]).astype(BF16)


def _out_proj(layer, ma, bo, gb, w, gpost, x, gffn):
    s = x.shape[0]
    tm = TM_PROJ
    row = lambda width: pl.BlockSpec((tm, width), lambda i: (i, 0))
    return pl.pallas_call(
        _out_proj_kernel,
        grid=(s // tm,),
        in_specs=[row(WIDTH), row(WIDTH), _resident((1, WIDTH), layer),
                  _resident((2 * WIDTH, D_MODEL), layer), _resident((1, D_MODEL), layer), row(D_MODEL),
                  _resident((1, D_MODEL), layer)],
        out_specs=[row(D_MODEL), row(D_MODEL)],
        out_shape=[jax.ShapeDtypeStruct((s, D_MODEL), F32), jax.ShapeDtypeStruct((s, D_MODEL), BF16)],
        compiler_params=_params(1),
        name="out_proj",
    )(ma, bo, gb, w, gpost, x, gffn)


def _ffn_kernel(h_ref, wg_ref, wu_ref, wd_ref, x1_ref, gpost_ref, o_ref):
    f = pl.program_id(1)

    @pl.when(f == 0)
    def _():
        o_ref[...] = jnp.zeros_like(o_ref)

    h = h_ref[...]
    act = (jax.nn.silu(_dot(h, wg_ref[...])) * _dot(h, wu_ref[...])).astype(BF16)
    o_ref[...] += _dot(act, wd_ref[...])

    @pl.when(f == pl.num_programs(1) - 1)
    def _():
        o_ref[...] = x1_ref[...] + _rms(o_ref[...], gpost_ref[...])


def _ffn(layer, h2, wg, wu, wd, x1, gpost):
    s = x1.shape[0]
    tm, tf = TM_FFN, TF_FFN
    row = pl.BlockSpec((tm, D_MODEL), lambda i, f: (i, 0))
    return pl.pallas_call(
        _ffn_kernel,
        grid=(s // tm, D_FF // tf),
        in_specs=[row, pl.BlockSpec((None, D_MODEL, tf), lambda i, f: (layer, 0, f)),
                  pl.BlockSpec((None, D_MODEL, tf), lambda i, f: (layer, 0, f)),
                  pl.BlockSpec((None, tf, D_MODEL), lambda i, f: (layer, f, 0)), row,
                  _resident((1, D_MODEL), layer)],
        out_specs=row,
        out_shape=jax.ShapeDtypeStruct((s, D_MODEL), F32),
        compiler_params=_params(2),
        name="ffn",
    )(h2, wg, wu, wd, x1, gpost)


def _rope_tables(s):
    half = ROT_DIM // 2
    inv_freq = ROPE_THETA ** (-jnp.arange(half, dtype=F32) / half)
    ang = jnp.arange(s, dtype=F32)[:, None] * inv_freq[None, :]
    cos, sin = jnp.cos(ang), jnp.sin(ang)
    zeros = jnp.zeros_like(sin)
    rest = HEAD_DIM - ROT_DIM
    cos_t = jnp.concatenate([cos, cos, jnp.ones((s, rest), F32)], axis=1)
    sina_t = jnp.concatenate([zeros, sin, jnp.zeros((s, rest), F32)], axis=1)
    sinb_t = jnp.concatenate([-sin, zeros, jnp.zeros((s, rest), F32)], axis=1)
    return cos_t, sina_t, sinb_t


def _trunk(x, p, depth):
    tables = _rope_tables(x.shape[0])
    for l in range(depth):
        ma, q, k, v = _in_proj(l, x, p["g_mix_pre"], p["w_in"], p["gmlp_v_gain"], p["gmlp_ws"],
                               p["gmlp_bias"], p["out_gain_a"], *tables)
        bo = _attn(q, k, v)
        x1, h2 = _out_proj(l, ma, bo, p["out_gain_b"], p["w_o"], p["g_mix_post"], x, p["g_ffn_pre"])
        x = _ffn(l, h2, p["w_gate"], p["w_up"], p["w_down"], x1, p["g_ffn_post"])
    return x


def kernel(x_prompt, x_sample, g_mix_pre, w_in, gmlp_v_gain, gmlp_ws, gmlp_bias, out_gain_a, out_gain_b,
           w_o, g_mix_post, g_ffn_pre, w_gate, w_up, w_down, g_ffn_post):
    depth = w_in.shape[0]
    row_vec = lambda t: t[:, None, :]
    p = {
        "g_mix_pre": row_vec(g_mix_pre),
        "w_in": w_in.astype(BF16),
        "gmlp_v_gain": row_vec(gmlp_v_gain),
        "gmlp_ws": gmlp_ws.astype(BF16),
        "gmlp_bias": gmlp_bias[..., None],
        "out_gain_a": row_vec(out_gain_a),
        "out_gain_b": row_vec(out_gain_b),
        "w_o": w_o.astype(BF16),
        "g_mix_post": row_vec(g_mix_post),
        "g_ffn_pre": row_vec(g_ffn_pre),
        "w_gate": w_gate.astype(BF16),
        "w_up": w_up.astype(BF16),
        "w_down": w_down.astype(BF16),
        "g_ffn_post": row_vec(g_ffn_post),
    }
    outs = []
    for x in (x_prompt, x_sample):
        b, s, dm = x.shape
        outs.append(jnp.stack([_trunk(x[i], p, depth) for i in range(b)]) if b > 1
                    else _trunk(x.reshape(s, dm), p, depth).reshape(1, s, dm))
    return tuple(outs)
```

```python
import functools

import jax
import jax.numpy as jnp
from jax import lax
from jax.experimental import pallas as pl
from jax.experimental.pallas import tpu as pltpu

D_MODEL = 2048
HEAD_DIM = 128
N_GROUPS = 8
WIDTH = N_GROUPS * HEAD_DIM
PROJ_WIDTH = 5 * WIDTH
CHUNK = 128
PATTERN_DILATIONS = (1, 4, 16)
RADIUS = 64
ROT_DIM = HEAD_DIM // 4
ROPE_THETA = 500000.0
D_FF = 5632
EPS = 1e-6
NEG_INF = -1e30
LOG2E = 1.4426950408889634

BF16 = jnp.bfloat16
F32 = jnp.float32

VMEM_LIMIT_BYTES = 56 * 1024 * 1024

TM_IN = 512
TM_IN_SUB = 256
TM_PROJ = 512
TM_SUB = 128
TM_FFN = 512
TF_FFN = 512
TQ_ATTN = 2048
HALO = RADIUS * PATTERN_DILATIONS[-1]
QBLK = 128
KBLK = QBLK + 2 * RADIUS
N_RES4 = 4
SUB4 = TQ_ATTN // N_RES4
HALO4 = HALO // N_RES4
BLOCKS_PER_RES4 = SUB4 // QBLK


def _rms(x, gain):
    return x * lax.rsqrt(jnp.mean(x * x, axis=-1, keepdims=True) + EPS) * gain


def _dot(a, b):
    return jnp.dot(a, b, preferred_element_type=F32)


def _resident(shape, layer):
    return pl.BlockSpec((None,) + shape, lambda *_: (layer,) + (0,) * len(shape),
                        pipeline_mode=pl.Buffered(1))


def _params(n_grid_axes):
    return pltpu.CompilerParams(
        dimension_semantics=("arbitrary",) * n_grid_axes,
        vmem_limit_bytes=VMEM_LIMIT_BYTES,
    )


def _spatial_gating(gu_ref, vn_ref, ws_ref, bias_ref, gain_ref, o_ref, row0, n_rows):
    for r0 in range(row0, row0 + n_rows, 2 * CHUNK):
        rows = (slice(r0, r0 + CHUNK), slice(r0 + CHUNK, r0 + 2 * CHUNK))
        gated = ([], [])
        for g in range(N_GROUPS):
            sl = slice(g * HEAD_DIM, (g + 1) * HEAD_DIM)
            vpair = jnp.concatenate([vn_ref[rows[0], sl], vn_ref[rows[1], sl]], axis=1)
            mixed = _dot(ws_ref[g], vpair) + bias_ref[g]
            gated[0].append(gu_ref[rows[0], sl] * mixed[:, :HEAD_DIM])
            gated[1].append(gu_ref[rows[1], sl] * mixed[:, HEAD_DIM:])
        for c in range(2):
            a = jnp.concatenate(gated[c], axis=1)
            o_ref[rows[c], :] = _rms(a, gain_ref[...]).astype(BF16)


def _in_proj_kernel(x_ref, g_ref, w_ref, vgain_ref, ws_ref, bias_ref, again_ref,
                    cos_ref, sina_ref, sinb_ref, ma_ref, q_ref, k_ref, v_ref, gu_s, vn_s):
    for r0 in range(0, x_ref.shape[0], TM_IN_SUB):
        rows = slice(r0, r0 + TM_IN_SUB)
        h = _rms(x_ref[rows, :], g_ref[...]).astype(BF16)

        gu_s[rows, :] = jax.nn.gelu(_dot(h, w_ref[:, 0:WIDTH]))

        va = jax.nn.gelu(_dot(h, w_ref[:, WIDTH:2 * WIDTH]))
        for g in range(N_GROUPS):
            sl = slice(g * HEAD_DIM, (g + 1) * HEAD_DIM)
            vg = va[:, sl]
            mu = jnp.mean(vg, axis=-1, keepdims=True)
            cen = vg - mu
            var = jnp.mean(cen * cen, axis=-1, keepdims=True)
            vn_s[rows, sl] = (cen * lax.rsqrt(var + EPS) * vgain_ref[:, sl]).astype(BF16)

        _spatial_gating(gu_s, vn_s, ws_ref, bias_ref, again_ref, ma_ref, r0, TM_IN_SUB)

        cos = cos_ref[rows, :]
        sina = sina_ref[rows, :]
        sinb = sinb_ref[rows, :]

        def rope_store(dst_ref, col0):
            y = _dot(h, w_ref[:, col0:col0 + WIDTH])
            for hd in range(N_GROUPS):
                sl = slice(hd * HEAD_DIM, (hd + 1) * HEAD_DIM)
                yh = y[:, sl]
                dst_ref[rows, sl] = (yh * cos
                                     + pltpu.roll(yh, ROT_DIM // 2, axis=1) * sina
                                     + pltpu.roll(yh, HEAD_DIM - ROT_DIM // 2, axis=1) * sinb)

        rope_store(q_ref, 2 * WIDTH)
        rope_store(k_ref, 3 * WIDTH)
        v_ref[rows, :] = _dot(h, w_ref[:, 4 * WIDTH:5 * WIDTH])


def _in_proj(layer, x, g, w, vgain, ws, bias, again, cos_t, sina_t, sinb_t):
    s = x.shape[0]
    tm = TM_IN
    row = lambda width: pl.BlockSpec((tm, width), lambda i: (i, 0))
    f32_out = jax.ShapeDtypeStruct((s, WIDTH), F32)
    return pl.pallas_call(
        _in_proj_kernel,
        grid=(s // tm,),
        in_specs=[row(D_MODEL), _resident((1, D_MODEL), layer), _resident((D_MODEL, PROJ_WIDTH), layer),
                  _resident((1, WIDTH), layer), _resident((N_GROUPS, CHUNK, CHUNK), layer),
                  _resident((N_GROUPS, CHUNK, 1), layer), _resident((1, WIDTH), layer),
                  row(HEAD_DIM), row(HEAD_DIM), row(HEAD_DIM)],
        out_specs=[row(WIDTH)] * 4,
        out_shape=[jax.ShapeDtypeStruct((s, WIDTH), BF16), f32_out, f32_out, f32_out],
        scratch_shapes=[pltpu.VMEM((tm, WIDTH), F32), pltpu.VMEM((tm, WIDTH), BF16)],
        compiler_params=_params(1),
        name="in_proj",
    )(x, g, w, vgain, ws, bias, again, cos_t, sina_t, sinb_t)


def _attn_kernel(q_ref, kp_ref, kc_ref, kn_ref, vp_ref, vc_ref, vn_ref, o_ref,
                 knat, vnat, q4, k4, v4, band, acc_s, m_s, l_s, *, seq_len):
    tile = pl.program_id(0)
    kv_refs = ((knat, k4, kp_ref, kc_ref, kn_ref), (vnat, v4, vp_ref, vc_ref, vn_ref))

    for nat, _, p_ref, c_ref, n_ref in kv_refs:
        nat[0:RADIUS, :] = p_ref[HALO - RADIUS:HALO, :]
        nat[RADIUS:RADIUS + TQ_ATTN, :] = c_ref[...]
        nat[RADIUS + TQ_ATTN:, :] = n_ref[0:RADIUS, :]
    for r in range(N_RES4):
        q4[r] = q_ref[pl.ds(r, SUB4, stride=N_RES4), :]
        for _, win4, p_ref, c_ref, n_ref in kv_refs:
            win4[r, 0:HALO4, :] = p_ref[pl.ds(r, HALO4, stride=N_RES4), :]
            win4[r, HALO4:HALO4 + SUB4, :] = c_ref[pl.ds(r, SUB4, stride=N_RES4), :]
            win4[r, HALO4 + SUB4:, :] = n_ref[pl.ds(r, HALO4, stride=N_RES4), :]

    qa = lax.broadcasted_iota(jnp.int32, (QBLK, KBLK), 0)
    kb = lax.broadcasted_iota(jnp.int32, (QBLK, KBLK), 1)
    band[...] = jnp.where((kb >= qa) & (kb <= qa + 2 * RADIUS), 0.0, NEG_INF).astype(F32)

    log2_scale = HEAD_DIM ** -0.5 * LOG2E
    ones_cols = jnp.ones((KBLK, HEAD_DIM), BF16)
    key_lane = lax.broadcasted_iota(jnp.int32, (1, KBLK), 1)

    def score_block(q, k, v, first_key_sub, sub_len):
        s = lax.dot_general(q.astype(BF16), k.astype(BF16), (((1,), (1,)), ((), ())),
                            preferred_element_type=F32) * log2_scale
        key_sub = first_key_sub + key_lane
        edge = jnp.where((key_sub >= 0) & (key_sub < sub_len), 0.0, NEG_INF).astype(F32)
        s = s + band[...] + edge
        m = jnp.max(s, axis=-1, keepdims=True)
        p = jnp.exp2(s - m).astype(BF16)
        pv = _dot(p, jnp.concatenate([v.astype(BF16), ones_cols], axis=1))
        return pv[:, :HEAD_DIM], pv[:, HEAD_DIM:], jnp.broadcast_to(m, (QBLK, HEAD_DIM))

    def keep(pattern, idx, result):
        acc_s[(pattern,) + idx] = result[0]
        l_s[(pattern,) + idx] = result[1]
        m_s[(pattern,) + idx] = result[2]

    def dilation1(g):
        for u in range(BLOCKS_PER_RES4):
            row0 = g * SUB4 + u * QBLK
            keys = pl.ds(row0, KBLK)
            res = score_block(q_ref[pl.ds(row0, QBLK), :], knat[keys, :], vnat[keys, :],
                              tile * TQ_ATTN + row0 - RADIUS, seq_len)
            keep(0, (g, slice(u * QBLK, (u + 1) * QBLK), slice(None)), res)

    def dilation4(r):
        for jb in range(BLOCKS_PER_RES4):
            l0 = jb * QBLK
            keys = slice(HALO4 - RADIUS + l0, HALO4 - RADIUS + l0 + KBLK)
            res = score_block(q4[r, l0:l0 + QBLK, :], k4[r, keys, :], v4[r, keys, :],
                              tile * SUB4 + l0 - RADIUS, seq_len // 4)
            keep(1, (r, slice(l0, l0 + QBLK), slice(None)), res)

    def dilation16(r):
        for jj in range(N_RES4):
            keys = pl.ds(jj, KBLK, stride=N_RES4)
            queries = pl.ds(jj, QBLK, stride=N_RES4)
            res = score_block(q4[r, queries, :], k4[r, keys, :], v4[r, keys, :],
                              tile * (TQ_ATTN // 16) - RADIUS, seq_len // 16)
            keep(2, (r, queries, slice(None)), res)

    def merge(r):
        for c in range(BLOCKS_PER_RES4):
            perm = (r, slice(c * QBLK, (c + 1) * QBLK), slice(None))
            nat = (c, pl.ds(r, QBLK, stride=N_RES4), slice(None))
            idx = ((0,) + nat, (1,) + perm, (2,) + perm)
            ms = [m_s[i] for i in idx]
            top = jnp.maximum(jnp.maximum(ms[0], ms[1]), ms[2])
            num = jnp.zeros((QBLK, HEAD_DIM), F32)
            den = jnp.zeros((QBLK, HEAD_DIM), F32)
            for i, m in zip(idx, ms):
                e = jnp.exp2(m - top)
                num = num + e * acc_s[i]
                den = den + e * l_s[i]
            o_ref[pl.ds(c * SUB4 + r, QBLK, stride=N_RES4), :] = num / den

    for phase in (dilation1, dilation4, dilation16, merge):
        for r in range(N_RES4):
            phase(r)


def _attn(q, k, v):
    s = q.shape[0]
    n_tiles = s // TQ_ATTN
    halo_per_tile = TQ_ATTN // HALO
    n_halo_blocks = s // HALO
    cur = pl.BlockSpec((TQ_ATTN, HEAD_DIM), lambda t, h: (t, h))
    prev = pl.BlockSpec((HALO, HEAD_DIM), lambda t, h: (jnp.maximum(t * halo_per_tile - 1, 0), h))
    nxt = pl.BlockSpec((HALO, HEAD_DIM),
                       lambda t, h: (jnp.minimum((t + 1) * halo_per_tile, n_halo_blocks - 1), h))
    nat_win = pltpu.VMEM((TQ_ATTN + 2 * RADIUS, HEAD_DIM), F32)
    win4 = pltpu.VMEM((N_RES4, SUB4 + 2 * HALO4, HEAD_DIM), F32)
    per_pattern = pltpu.VMEM((len(PATTERN_DILATIONS), N_RES4, SUB4, HEAD_DIM), F32)
    return pl.pallas_call(
        functools.partial(_attn_kernel, seq_len=s),
        grid=(n_tiles, N_GROUPS),
        in_specs=[cur, prev, cur, nxt, prev, cur, nxt],
        out_specs=cur,
        out_shape=jax.ShapeDtypeStruct((s, WIDTH), F32),
        scratch_shapes=[nat_win, nat_win, pltpu.VMEM((N_RES4, SUB4, HEAD_DIM), F32), win4, win4,
                        pltpu.VMEM((QBLK, KBLK), F32), per_pattern, per_pattern, per_pattern],
        compiler_params=_params(2),
        name="attn",
    )(q, k, k, k, v, v, v)


def _out_proj_kernel(ma_ref, bo_ref, gb_ref, w_ref, gpost_ref, x_ref, gffn_ref, x1_ref, h2_ref):
    for r0 in range(0, ma_ref.shape[0], TM_SUB):
        rows = slice(r0, r0 + TM_SUB)
        mb = _rms(bo_ref[rows, :], gb_ref[...]).astype(BF16)
        mix = _dot(jnp.concatenate([ma_ref[rows, :], mb], axis=1), w_ref[...])
        x1 = x_ref[rows, :] + _rms(mix, gpost_ref[...])
        x1_ref[rows, :] = x1
        h2_ref[rows, :] = _rms(x1, gffn_ref[...]).astype(BF16)


def _out_proj(layer, ma, bo, gb, w, gpost, x, gffn):
    s = x.shape[0]
    tm = TM_PROJ
    row = lambda width: pl.BlockSpec((tm, width), lambda i: (i, 0))
    return pl.pallas_call(
        _out_proj_kernel,
        grid=(s // tm,),
        in_specs=[row(WIDTH), row(WIDTH), _resident((1, WIDTH), layer),
                  _resident((2 * WIDTH, D_MODEL), layer), _resident((1, D_MODEL), layer), row(D_MODEL),
                  _resident((1, D_MODEL), layer)],
        out_specs=[row(D_MODEL), row(D_MODEL)],
        out_shape=[jax.ShapeDtypeStruct((s, D_MODEL), F32), jax.ShapeDtypeStruct((s, D_MODEL), BF16)],
        compiler_params=_params(1),
        name="out_proj",
    )(ma, bo, gb, w, gpost, x, gffn)


def _ffn_kernel(h_ref, wg_ref, wu_ref, wd_ref, x1_ref, gpost_ref, o_ref):
    f = pl.program_id(1)

    @pl.when(f == 0)
    def _():
        o_ref[...] = jnp.zeros_like(o_ref)

    h = h_ref[...]
    act = (jax.nn.silu(_dot(h, wg_ref[...])) * _dot(h, wu_ref[...])).astype(BF16)
    o_ref[...] += _dot(act, wd_ref[...])

    @pl.when(f == pl.num_programs(1) - 1)
    def _():
        o_ref[...] = x1_ref[...] + _rms(o_ref[...], gpost_ref[...])


def _ffn(layer, h2, wg, wu, wd, x1, gpost):
    s = x1.shape[0]
    tm, tf = TM_FFN, TF_FFN
    row = pl.BlockSpec((tm, D_MODEL), lambda i, f: (i, 0))
    return pl.pallas_call(
        _ffn_kernel,
        grid=(s // tm, D_FF // tf),
        in_specs=[row, pl.BlockSpec((None, D_MODEL, tf), lambda i, f: (layer, 0, f)),
                  pl.BlockSpec((None, D_MODEL, tf), lambda i, f: (layer, 0, f)),
                  pl.BlockSpec((None, tf, D_MODEL), lambda i, f: (layer, f, 0)), row,
                  _resident((1, D_MODEL), layer)],
        out_specs=row,
        out_shape=jax.ShapeDtypeStruct((s, D_MODEL), F32),
        compiler_params=_params(2),
        name="ffn",
    )(h2, wg, wu, wd, x1, gpost)


def _rope_tables(s):
    half = ROT_DIM // 2
    inv_freq = ROPE_THETA ** (-jnp.arange(half, dtype=F32) / half)
    ang = jnp.arange(s, dtype=F32)[:, None] * inv_freq[None, :]
    cos, sin = jnp.cos(ang), jnp.sin(ang)
    zeros = jnp.zeros_like(sin)
    rest = HEAD_DIM - ROT_DIM
    cos_t = jnp.concatenate([cos, cos, jnp.ones((s, rest), F32)], axis=1)
    sina_t = jnp.concatenate([zeros, sin, jnp.zeros((s, rest), F32)], axis=1)
    sinb_t = jnp.concatenate([-sin, zeros, jnp.zeros((s, rest), F32)], axis=1)
    return cos_t, sina_t, sinb_t


def _trunk(x, p, depth, tables):
    for l in range(depth):
        ma, q, k, v = _in_proj(l, x, p["g_mix_pre"], p["w_in"], p["gmlp_v_gain"], p["gmlp_ws"],
                               p["gmlp_bias"], p["out_gain_a"], *tables)
        bo = _attn(q, k, v)
        x1, h2 = _out_proj(l, ma, bo, p["out_gain_b"], p["w_o"], p["g_mix_post"], x, p["g_ffn_pre"])
        x = _ffn(l, h2, p["w_gate"], p["w_up"], p["w_down"], x1, p["g_ffn_post"])
    return x


def kernel(x_prompt, x_sample, g_mix_pre, w_in, gmlp_v_gain, gmlp_ws, gmlp_bias, out_gain_a, out_gain_b,
           w_o, g_mix_post, g_ffn_pre, w_gate, w_up, w_down, g_ffn_post):
    depth = w_in.shape[0]
    row_vec = lambda t: t[:, None, :]
    p = {
        "g_mix_pre": row_vec(g_mix_pre),
        "w_in": w_in.astype(BF16),
        "gmlp_v_gain": row_vec(gmlp_v_gain),
        "gmlp_ws": gmlp_ws.astype(BF16),
        "gmlp_bias": gmlp_bias[..., None],
        "out_gain_a": row_vec(out_gain_a),
        "out_gain_b": row_vec(out_gain_b),
        "w_o": w_o.astype(BF16),
        "g_mix_post": row_vec(g_mix_post),
        "g_ffn_pre": row_vec(g_ffn_pre),
        "w_gate": w_gate.astype(BF16),
        "w_up": w_up.astype(BF16),
        "w_down": w_down.astype(BF16),
        "g_ffn_post": row_vec(g_ffn_post),
    }
    tables = _rope_tables(max(x_prompt.shape[1], x_sample.shape[1]))
    outs = []
    for x in (x_prompt, x_sample):
        b, s, dm = x.shape
        outs.append(jnp.stack([_trunk(x[i], p, depth, tables) for i in range(b)]) if b > 1
                    else _trunk(x.reshape(s, dm), p, depth, tables).reshape(1, s, dm))
    return tuple(outs)
```

```python
import functools

import jax
import jax.numpy as jnp
from jax import lax
from jax.experimental import pallas as pl
from jax.experimental.pallas import tpu as pltpu

D_MODEL = 2048
HEAD_DIM = 128
N_GROUPS = 8
WIDTH = N_GROUPS * HEAD_DIM
PROJ_WIDTH = 5 * WIDTH
CHUNK = 128
PATTERN_DILATIONS = (1, 4, 16)
RADIUS = 64
ROT_DIM = HEAD_DIM // 4
ROPE_THETA = 500000.0
D_FF = 5632
EPS = 1e-6
NEG_INF = -1e30
LOG2E = 1.4426950408889634

BF16 = jnp.bfloat16
F32 = jnp.float32

VMEM_LIMIT_BYTES = 56 * 1024 * 1024

TM_IN = 512
TM_IN_SUB = 512
TM_PROJ = 512
TM_SUB = 128
TM_FFN = 512
TF_FFN = 512
TQ_ATTN = 2048
HALO = RADIUS * PATTERN_DILATIONS[-1]
QBLK = 128
KBLK = QBLK + 2 * RADIUS
N_RES4 = 4
SUB4 = TQ_ATTN // N_RES4
HALO4 = HALO // N_RES4
BLOCKS_PER_RES4 = SUB4 // QBLK


def _rms(x, gain):
    return x * lax.rsqrt(jnp.mean(x * x, axis=-1, keepdims=True) + EPS) * gain


def _dot(a, b):
    return jnp.dot(a, b, preferred_element_type=F32)


def _resident(shape, layer):
    return pl.BlockSpec((None,) + shape, lambda *_: (layer,) + (0,) * len(shape),
                        pipeline_mode=pl.Buffered(1))


def _params(n_grid_axes):
    return pltpu.CompilerParams(
        dimension_semantics=("arbitrary",) * n_grid_axes,
        vmem_limit_bytes=VMEM_LIMIT_BYTES,
    )


def _spatial_gating(gu_ref, vn_ref, ws_ref, bias_ref, gain_ref, o_ref, row0, n_rows):
    for r0 in range(row0, row0 + n_rows, 2 * CHUNK):
        rows = (slice(r0, r0 + CHUNK), slice(r0 + CHUNK, r0 + 2 * CHUNK))
        gated = ([], [])
        for g in range(N_GROUPS):
            sl = slice(g * HEAD_DIM, (g + 1) * HEAD_DIM)
            vpair = jnp.concatenate([vn_ref[rows[0], sl], vn_ref[rows[1], sl]], axis=1)
            mixed = _dot(ws_ref[g], vpair) + bias_ref[g]
            gated[0].append(gu_ref[rows[0], sl] * mixed[:, :HEAD_DIM])
            gated[1].append(gu_ref[rows[1], sl] * mixed[:, HEAD_DIM:])
        for c in range(2):
            a = jnp.concatenate(gated[c], axis=1)
            o_ref[rows[c], :] = _rms(a, gain_ref[...]).astype(BF16)


def _in_proj_kernel(x_ref, g_ref, w_ref, vgain_ref, ws_ref, bias_ref, again_ref,
                    cos_ref, sina_ref, sinb_ref, ma_ref, q_ref, k_ref, v_ref, gu_s, vn_s):
    for r0 in range(0, x_ref.shape[0], TM_IN_SUB):
        rows = slice(r0, r0 + TM_IN_SUB)
        h = _rms(x_ref[rows, :], g_ref[...]).astype(BF16)

        gu_s[rows, :] = jax.nn.gelu(_dot(h, w_ref[:, 0:WIDTH]))

        va = jax.nn.gelu(_dot(h, w_ref[:, WIDTH:2 * WIDTH]))
        for g in range(N_GROUPS):
            sl = slice(g * HEAD_DIM, (g + 1) * HEAD_DIM)
            vg = va[:, sl]
            mu = jnp.mean(vg, axis=-1, keepdims=True)
            cen = vg - mu
            var = jnp.mean(cen * cen, axis=-1, keepdims=True)
            vn_s[rows, sl] = (cen * lax.rsqrt(var + EPS) * vgain_ref[:, sl]).astype(BF16)

        _spatial_gating(gu_s, vn_s, ws_ref, bias_ref, again_ref, ma_ref, r0, TM_IN_SUB)

        cos = cos_ref[rows, :]
        sina = sina_ref[rows, :]
        sinb = sinb_ref[rows, :]

        def rope_store(dst_ref, col0):
            y = _dot(h, w_ref[:, col0:col0 + WIDTH])
            for hd in range(N_GROUPS):
                sl = slice(hd * HEAD_DIM, (hd + 1) * HEAD_DIM)
                yh = y[:, sl]
                dst_ref[rows, sl] = (yh * cos
                                     + pltpu.roll(yh, ROT_DIM // 2, axis=1) * sina
                                     + pltpu.roll(yh, HEAD_DIM - ROT_DIM // 2, axis=1) * sinb)

        rope_store(q_ref, 2 * WIDTH)
        rope_store(k_ref, 3 * WIDTH)
        v_ref[rows, :] = _dot(h, w_ref[:, 4 * WIDTH:5 * WIDTH])


def _in_proj(layer, x, g, w, vgain, ws, bias, again, cos_t, sina_t, sinb_t):
    s = x.shape[0]
    tm = TM_IN
    row = lambda width: pl.BlockSpec((tm, width), lambda i: (i, 0))
    f32_out = jax.ShapeDtypeStruct((s, WIDTH), F32)
    return pl.pallas_call(
        _in_proj_kernel,
        grid=(s // tm,),
        in_specs=[row(D_MODEL), _resident((1, D_MODEL), layer), _resident((D_MODEL, PROJ_WIDTH), layer),
                  _resident((1, WIDTH), layer), _resident((N_GROUPS, CHUNK, CHUNK), layer),
                  _resident((N_GROUPS, CHUNK, 1), layer), _resident((1, WIDTH), layer),
                  row(HEAD_DIM), row(HEAD_DIM), row(HEAD_DIM)],
        out_specs=[row(WIDTH)] * 4,
        out_shape=[jax.ShapeDtypeStruct((s, WIDTH), BF16), f32_out, f32_out, f32_out],
        scratch_shapes=[pltpu.VMEM((tm, WIDTH), F32), pltpu.VMEM((tm, WIDTH), BF16)],
        compiler_params=_params(1),
        name="in_proj",
    )(x, g, w, vgain, ws, bias, again, cos_t, sina_t, sinb_t)


def _attn_kernel(q_ref, kp_ref, kc_ref, kn_ref, vp_ref, vc_ref, vn_ref, o_ref,
                 q4, k4, v4, band, acc_s, m_s, l_s, *, seq_len):
    tile = pl.program_id(0)

    for r in range(N_RES4):
        q4[r] = q_ref[pl.ds(r, SUB4, stride=N_RES4), :]
        for win4, p_ref, c_ref, n_ref in ((k4, kp_ref, kc_ref, kn_ref), (v4, vp_ref, vc_ref, vn_ref)):
            win4[r, 0:HALO4, :] = p_ref[pl.ds(r, HALO4, stride=N_RES4), :]
            win4[r, HALO4:HALO4 + SUB4, :] = c_ref[pl.ds(r, SUB4, stride=N_RES4), :]
            win4[r, HALO4 + SUB4:, :] = n_ref[pl.ds(r, HALO4, stride=N_RES4), :]

    qa = lax.broadcasted_iota(jnp.int32, (QBLK, KBLK), 0)
    kb = lax.broadcasted_iota(jnp.int32, (QBLK, KBLK), 1)
    band[...] = jnp.where((kb >= qa) & (kb <= qa + 2 * RADIUS), 0.0, NEG_INF).astype(F32)

    log2_scale = HEAD_DIM ** -0.5 * LOG2E
    ones_cols = jnp.ones((KBLK, HEAD_DIM), BF16)
    key_lane = lax.broadcasted_iota(jnp.int32, (1, KBLK), 1)

    def score_block(q, k, v, first_key_sub, sub_len, may_cross_end):
        s = lax.dot_general(q.astype(BF16), k.astype(BF16), (((1,), (1,)), ((), ())),
                            preferred_element_type=F32) * log2_scale
        s = s + band[...]
        if may_cross_end:
            key_sub = first_key_sub + key_lane
            s = s + jnp.where((key_sub >= 0) & (key_sub < sub_len), 0.0, NEG_INF).astype(F32)
        m = jnp.max(s, axis=-1, keepdims=True)
        p = jnp.exp2(s - m).astype(BF16)
        pv = _dot(p, jnp.concatenate([v.astype(BF16), ones_cols], axis=1))
        return pv[:, :HEAD_DIM], pv[:, HEAD_DIM:], jnp.broadcast_to(m, (QBLK, HEAD_DIM))

    def keep(pattern, idx, result):
        acc_s[(pattern,) + idx] = result[0]
        l_s[(pattern,) + idx] = result[1]
        m_s[(pattern,) + idx] = result[2]

    def natural_window(p_ref, c_ref, n_ref, row0):
        if row0 == 0:
            return jnp.concatenate([p_ref[HALO - RADIUS:HALO, :], c_ref[0:KBLK - RADIUS, :]], axis=0)
        if row0 == TQ_ATTN - QBLK:
            return jnp.concatenate([c_ref[row0 - RADIUS:TQ_ATTN, :], n_ref[0:RADIUS, :]], axis=0)
        return c_ref[row0 - RADIUS:row0 - RADIUS + KBLK, :]

    def dilation1(g):
        for u in range(BLOCKS_PER_RES4):
            row0 = g * SUB4 + u * QBLK
            res = score_block(q_ref[row0:row0 + QBLK, :], natural_window(kp_ref, kc_ref, kn_ref, row0),
                              natural_window(vp_ref, vc_ref, vn_ref, row0),
                              tile * TQ_ATTN + row0 - RADIUS, seq_len,
                              may_cross_end=row0 in (0, TQ_ATTN - QBLK))
            keep(0, (g, slice(u * QBLK, (u + 1) * QBLK), slice(None)), res)

    def dilation4(r):
        for jb in range(BLOCKS_PER_RES4):
            l0 = jb * QBLK
            keys = slice(HALO4 - RADIUS + l0, HALO4 - RADIUS + l0 + KBLK)
            res = score_block(q4[r, l0:l0 + QBLK, :], k4[r, keys, :], v4[r, keys, :],
                              tile * SUB4 + l0 - RADIUS, seq_len // 4,
                              may_cross_end=jb in (0, BLOCKS_PER_RES4 - 1))
            keep(1, (r, slice(l0, l0 + QBLK), slice(None)), res)

    def dilation16(r):
        for jj in range(N_RES4):
            keys = pl.ds(jj, KBLK, stride=N_RES4)
            queries = pl.ds(jj, QBLK, stride=N_RES4)
            res = score_block(q4[r, queries, :], k4[r, keys, :], v4[r, keys, :],
                              tile * (TQ_ATTN // 16) - RADIUS, seq_len // 16, may_cross_end=True)
            keep(2, (r, queries, slice(None)), res)

    def merge(r):
        for c in range(BLOCKS_PER_RES4):
            perm = (r, slice(c * QBLK, (c + 1) * QBLK), slice(None))
            nat = (c, pl.ds(r, QBLK, stride=N_RES4), slice(None))
            idx = ((0,) + nat, (1,) + perm, (2,) + perm)
            ms = [m_s[i] for i in idx]
            top = jnp.maximum(jnp.maximum(ms[0], ms[1]), ms[2])
            num = jnp.zeros((QBLK, HEAD_DIM), F32)
            den = jnp.zeros((QBLK, HEAD_DIM), F32)
            for i, m in zip(idx, ms):
                e = jnp.exp2(m - top)
                num = num + e * acc_s[i]
                den = den + e * l_s[i]
            o_ref[pl.ds(c * SUB4 + r, QBLK, stride=N_RES4), :] = num / den

    for phase in (dilation1, dilation4, dilation16, merge):
        for r in range(N_RES4):
            phase(r)


def _attn(q, k, v):
    s = q.shape[0]
    n_tiles = s // TQ_ATTN
    halo_per_tile = TQ_ATTN // HALO
    n_halo_blocks = s // HALO
    cur = pl.BlockSpec((TQ_ATTN, HEAD_DIM), lambda t, h: (t, h))
    prev = pl.BlockSpec((HALO, HEAD_DIM), lambda t, h: (jnp.maximum(t * halo_per_tile - 1, 0), h))
    nxt = pl.BlockSpec((HALO, HEAD_DIM),
                       lambda t, h: (jnp.minimum((t + 1) * halo_per_tile, n_halo_blocks - 1), h))
    win4 = pltpu.VMEM((N_RES4, SUB4 + 2 * HALO4, HEAD_DIM), F32)
    per_pattern = pltpu.VMEM((len(PATTERN_DILATIONS), N_RES4, SUB4, HEAD_DIM), F32)
    return pl.pallas_call(
        functools.partial(_attn_kernel, seq_len=s),
        grid=(n_tiles, N_GROUPS),
        in_specs=[cur, prev, cur, nxt, prev, cur, nxt],
        out_specs=cur,
        out_shape=jax.ShapeDtypeStruct((s, WIDTH), F32),
        scratch_shapes=[pltpu.VMEM((N_RES4, SUB4, HEAD_DIM), F32), win4, win4,
                        pltpu.VMEM((QBLK, KBLK), F32), per_pattern, per_pattern, per_pattern],
        compiler_params=_params(2),
        name="attn",
    )(q, k, k, k, v, v, v)


def _out_proj_kernel(ma_ref, bo_ref, gb_ref, w_ref, gpost_ref, x_ref, gffn_ref, x1_ref, h2_ref):
    for r0 in range(0, ma_ref.shape[0], TM_SUB):
        rows = slice(r0, r0 + TM_SUB)
        mb = _rms(bo_ref[rows, :], gb_ref[...]).astype(BF16)
        mix = _dot(jnp.concatenate([ma_ref[rows, :], mb], axis=1), w_ref[...])
        x1 = x_ref[rows, :] + _rms(mix, gpost_ref[...])
        x1_ref[rows, :] = x1
        h2_ref[rows, :] = _rms(x1, gffn_ref[...]).astype(BF16)


def _out_proj(layer, ma, bo, gb, w, gpost, x, gffn):
    s = x.shape[0]
    tm = TM_PROJ
    row = lambda width: pl.BlockSpec((tm, width), lambda i: (i, 0))
    return pl.pallas_call(
        _out_proj_kernel,
        grid=(s // tm,),
        in_specs=[row(WIDTH), row(WIDTH), _resident((1, WIDTH), layer),
                  _resident((2 * WIDTH, D_MODEL), layer), _resident((1, D_MODEL), layer), row(D_MODEL),
                  _resident((1, D_MODEL), layer)],
        out_specs=[row(D_MODEL), row(D_MODEL)],
        out_shape=[jax.ShapeDtypeStruct((s, D_MODEL), F32), jax.ShapeDtypeStruct((s, D_MODEL), BF16)],
        compiler_params=_params(1),
        name="out_proj",
    )(ma, bo, gb, w, gpost, x, gffn)


def _ffn_kernel(h_ref, wg_ref, wu_ref, wd_ref, x1_ref, gpost_ref, o_ref):
    f = pl.program_id(1)
    last = pl.num_programs(1) - 1

    def partial_sum(rows):
        h = h_ref[rows, :]
        act = (jax.nn.silu(_dot(h, wg_ref[...])) * _dot(h, wu_ref[...])).astype(BF16)
        return _dot(act, wd_ref[...])

    everything = slice(None)

    @pl.when(f == 0)
    def _():
        o_ref[...] = partial_sum(everything)

    @pl.when((f > 0) & (f < last))
    def _():
        o_ref[...] += partial_sum(everything)

    @pl.when(f == last)
    def _():
        half = o_ref.shape[0] // 2
        for r0 in (0, half):
            rows = slice(r0, r0 + half)
            total = o_ref[rows, :] + partial_sum(rows)
            o_ref[rows, :] = x1_ref[rows, :] + _rms(total, gpost_ref[...])


def _ffn(layer, h2, wg, wu, wd, x1, gpost):
    s = x1.shape[0]
    tm, tf = TM_FFN, TF_FFN
    assert D_FF // tf >= 2
    row = pl.BlockSpec((tm, D_MODEL), lambda i, f: (i, 0))
    return pl.pallas_call(
        _ffn_kernel,
        grid=(s // tm, D_FF // tf),
        in_specs=[row, pl.BlockSpec((None, D_MODEL, tf), lambda i, f: (layer, 0, f)),
                  pl.BlockSpec((None, D_MODEL, tf), lambda i, f: (layer, 0, f)),
                  pl.BlockSpec((None, tf, D_MODEL), lambda i, f: (layer, f, 0)), row,
                  _resident((1, D_MODEL), layer)],
        out_specs=row,
        out_shape=jax.ShapeDtypeStruct((s, D_MODEL), F32),
        compiler_params=_params(2),
        name="ffn",
    )(h2, wg, wu, wd, x1, gpost)


def _rope_tables(s):
    half = ROT_DIM // 2
    inv_freq = ROPE_THETA ** (-jnp.arange(half, dtype=F32) / half)
    ang = jnp.arange(s, dtype=F32)[:, None] * inv_freq[None, :]
    cos, sin = jnp.cos(ang), jnp.sin(ang)
    zeros = jnp.zeros_like(sin)
    rest = HEAD_DIM - ROT_DIM
    cos_t = jnp.concatenate([cos, cos, jnp.ones((s, rest), F32)], axis=1)
    sina_t = jnp.concatenate([zeros, sin, jnp.zeros((s, rest), F32)], axis=1)
    sinb_t = jnp.concatenate([-sin, zeros, jnp.zeros((s, rest), F32)], axis=1)
    return cos_t, sina_t, sinb_t


def _trunk(x, p, depth, tables):
    for l in range(depth):
        ma, q, k, v = _in_proj(l, x, p["g_mix_pre"], p["w_in"], p["gmlp_v_gain"], p["gmlp_ws"],
                               p["gmlp_bias"], p["out_gain_a"], *tables)
        bo = _attn(q, k, v)
        x1, h2 = _out_proj(l, ma, bo, p["out_gain_b"], p["w_o"], p["g_mix_post"], x, p["g_ffn_pre"])
        x = _ffn(l, h2, p["w_gate"], p["w_up"], p["w_down"], x1, p["g_ffn_post"])
    return x


def kernel(x_prompt, x_sample, g_mix_pre, w_in, gmlp_v_gain, gmlp_ws, gmlp_bias, out_gain_a, out_gain_b,
           w_o, g_mix_post, g_ffn_pre, w_gate, w_up, w_down, g_ffn_post):
    depth = w_in.shape[0]
    row_vec = lambda t: t[:, None, :]
    p = {
        "g_mix_pre": row_vec(g_mix_pre),
        "w_in": w_in.astype(BF16),
        "gmlp_v_gain": row_vec(gmlp_v_gain),
        "gmlp_ws": gmlp_ws.astype(BF16),
        "gmlp_bias": gmlp_bias[..., None],
        "out_gain_a": row_vec(out_gain_a),
        "out_gain_b": row_vec(out_gain_b),
        "w_o": w_o.astype(BF16),
        "g_mix_post": row_vec(g_mix_post),
        "g_ffn_pre": row_vec(g_ffn_pre),
        "w_gate": w_gate.astype(BF16),
        "w_up": w_up.astype(BF16),
        "w_down": w_down.astype(BF16),
        "g_ffn_post": row_vec(g_ffn_post),
    }
    tables = _rope_tables(max(x_prompt.shape[1], x_sample.shape[1]))
    outs = []
    for x in (x_prompt, x_sample):
        b, s, dm = x.shape
        outs.append(jnp.stack([_trunk(x[i], p, depth, tables) for i in range(b)]) if b > 1
                    else _trunk(x.reshape(s, dm), p, depth, tables).reshape(1, s, dm))
    return tuple(outs)
```

```python
import functools

import jax
import jax.numpy as jnp
from jax import lax
from jax.experimental import pallas as pl
from jax.experimental.pallas import tpu as pltpu

D_MODEL = 2048
HEAD_DIM = 128
N_GROUPS = 8
WIDTH = N_GROUPS * HEAD_DIM
PROJ_WIDTH = 5 * WIDTH
CHUNK = 128
PATTERN_DILATIONS = (1, 4, 16)
RADIUS = 64
ROT_DIM = HEAD_DIM // 4
ROPE_THETA = 500000.0
D_FF = 5632
EPS = 1e-6
NEG_INF = -1e30
LOG2E = 1.4426950408889634

BF16 = jnp.bfloat16
F32 = jnp.float32

VMEM_LIMIT_BYTES = 56 * 1024 * 1024
VMEM_LIMIT_FFN_BYTES = 62 * 1024 * 1024

TM_IN = 512
TM_IN_SUB = 512
TM_PROJ = 512
TM_SUB = 128
TM_FFN = 1024
TF_FFN = 512
TQ_ATTN = 2048
HALO = RADIUS * PATTERN_DILATIONS[-1]
QBLK = 128
KBLK = QBLK + 2 * RADIUS
N_RES4 = 4
SUB4 = TQ_ATTN // N_RES4
HALO4 = HALO // N_RES4
BLOCKS_PER_RES4 = SUB4 // QBLK


def _rms(x, gain):
    return x * lax.rsqrt(jnp.mean(x * x, axis=-1, keepdims=True) + EPS) * gain


def _dot(a, b):
    return jnp.dot(a, b, preferred_element_type=F32)


def _resident(shape, layer):
    return pl.BlockSpec((None,) + shape, lambda *_: (layer,) + (0,) * len(shape),
                        pipeline_mode=pl.Buffered(1))


def _whole(shape):
    return pl.BlockSpec(shape, lambda *_: (0,) * len(shape), pipeline_mode=pl.Buffered(1))


def _params(n_grid_axes, vmem_limit_bytes=VMEM_LIMIT_BYTES):
    return pltpu.CompilerParams(
        dimension_semantics=("arbitrary",) * n_grid_axes,
        vmem_limit_bytes=vmem_limit_bytes,
    )


def _spatial_gating(gu_ref, vn_ref, ws_ref, bias_ref, gain_ref, o_ref, row0, n_rows):
    for r0 in range(row0, row0 + n_rows, 2 * CHUNK):
        rows = (slice(r0, r0 + CHUNK), slice(r0 + CHUNK, r0 + 2 * CHUNK))
        gated = ([], [])
        for g in range(N_GROUPS):
            sl = slice(g * HEAD_DIM, (g + 1) * HEAD_DIM)
            vpair = jnp.concatenate([vn_ref[rows[0], sl], vn_ref[rows[1], sl]], axis=1)
            mixed = _dot(ws_ref[g], vpair) + bias_ref[g]
            gated[0].append(gu_ref[rows[0], sl] * mixed[:, :HEAD_DIM])
            gated[1].append(gu_ref[rows[1], sl] * mixed[:, HEAD_DIM:])
        for c in range(2):
            a = jnp.concatenate(gated[c], axis=1)
            o_ref[rows[c], :] = _rms(a, gain_ref[...]).astype(BF16)


def _in_proj_kernel(x_ref, g_ref, w_ref, vgain_ref, ws_ref, bias_ref, again_ref,
                    cos_ref, sina_ref, sinb_ref, ma_ref, q_ref, k_ref, v_ref, gu_s, vn_s):
    for r0 in range(0, x_ref.shape[0], TM_IN_SUB):
        rows = slice(r0, r0 + TM_IN_SUB)
        h = _rms(x_ref[rows, :], g_ref[...]).astype(BF16)

        gu_s[rows, :] = jax.nn.gelu(_dot(h, w_ref[:, 0:WIDTH]))

        va = jax.nn.gelu(_dot(h, w_ref[:, WIDTH:2 * WIDTH]))
        for g in range(N_GROUPS):
            sl = slice(g * HEAD_DIM, (g + 1) * HEAD_DIM)
            vg = va[:, sl]
            mu = jnp.mean(vg, axis=-1, keepdims=True)
            cen = vg - mu
            var = jnp.mean(cen * cen, axis=-1, keepdims=True)
            vn_s[rows, sl] = (cen * lax.rsqrt(var + EPS) * vgain_ref[:, sl]).astype(BF16)

        _spatial_gating(gu_s, vn_s, ws_ref, bias_ref, again_ref, ma_ref, r0, TM_IN_SUB)

        cos = cos_ref[rows, :]
        sina = sina_ref[rows, :]
        sinb = sinb_ref[rows, :]

        def rope_store(dst_ref, col0):
            y = _dot(h, w_ref[:, col0:col0 + WIDTH])
            for hd in range(N_GROUPS):
                sl = slice(hd * HEAD_DIM, (hd + 1) * HEAD_DIM)
                yh = y[:, sl]
                dst_ref[rows, sl] = (yh * cos
                                     + pltpu.roll(yh, ROT_DIM // 2, axis=1) * sina
                                     + pltpu.roll(yh, HEAD_DIM - ROT_DIM // 2, axis=1) * sinb)

        rope_store(q_ref, 2 * WIDTH)
        rope_store(k_ref, 3 * WIDTH)
        v_ref[rows, :] = _dot(h, w_ref[:, 4 * WIDTH:5 * WIDTH])


def _in_proj(layer, x, g, w, vgain, ws, bias, again, cos_t, sina_t, sinb_t):
    s = x.shape[0]
    tm = TM_IN
    row = lambda width: pl.BlockSpec((tm, width), lambda i: (i, 0))
    f32_out = jax.ShapeDtypeStruct((s, WIDTH), F32)
    return pl.pallas_call(
        _in_proj_kernel,
        grid=(s // tm,),
        in_specs=[row(D_MODEL), _resident((1, D_MODEL), layer), _whole((D_MODEL, PROJ_WIDTH)),
                  _resident((1, WIDTH), layer), _resident((N_GROUPS, CHUNK, CHUNK), layer),
                  _resident((N_GROUPS, CHUNK, 1), layer), _resident((1, WIDTH), layer),
                  row(HEAD_DIM), row(HEAD_DIM), row(HEAD_DIM)],
        out_specs=[row(WIDTH)] * 4,
        out_shape=[jax.ShapeDtypeStruct((s, WIDTH), BF16), f32_out, f32_out, f32_out],
        scratch_shapes=[pltpu.VMEM((tm, WIDTH), F32), pltpu.VMEM((tm, WIDTH), BF16)],
        compiler_params=_params(1),
        name="in_proj",
    )(x, g, w, vgain, ws, bias, again, cos_t, sina_t, sinb_t)


def _attn_kernel(*refs, seq_len, n_cast):
    q_ref, kp_ref, kc_ref, kn_ref, vp_ref, vc_ref, vn_ref = refs[:7]
    o_ref = refs[7 + n_cast]
    q4, k4, v4, band, acc_s, m_s, l_s = refs[8 + 2 * n_cast:]
    for src, dst in zip(refs[7:7 + n_cast], refs[8 + n_cast:8 + 2 * n_cast]):
        dst[...] = src[...].astype(BF16)

    tile = pl.program_id(0)

    for r in range(N_RES4):
        q4[r] = q_ref[pl.ds(r, SUB4, stride=N_RES4), :]
        for win4, p_ref, c_ref, n_ref in ((k4, kp_ref, kc_ref, kn_ref), (v4, vp_ref, vc_ref, vn_ref)):
            win4[r, 0:HALO4, :] = p_ref[pl.ds(r, HALO4, stride=N_RES4), :]
            win4[r, HALO4:HALO4 + SUB4, :] = c_ref[pl.ds(r, SUB4, stride=N_RES4), :]
            win4[r, HALO4 + SUB4:, :] = n_ref[pl.ds(r, HALO4, stride=N_RES4), :]

    qa = lax.broadcasted_iota(jnp.int32, (QBLK, KBLK), 0)
    kb = lax.broadcasted_iota(jnp.int32, (QBLK, KBLK), 1)
    band[...] = jnp.where((kb >= qa) & (kb <= qa + 2 * RADIUS), 0.0, NEG_INF).astype(F32)

    log2_scale = HEAD_DIM ** -0.5 * LOG2E
    ones_cols = jnp.ones((KBLK, HEAD_DIM), BF16)
    key_lane = lax.broadcasted_iota(jnp.int32, (1, KBLK), 1)

    def score_block(q, k, v, first_key_sub, sub_len, may_cross_end):
        s = lax.dot_general(q.astype(BF16), k.astype(BF16), (((1,), (1,)), ((), ())),
                            preferred_element_type=F32) * log2_scale
        s = s + band[...]
        if may_cross_end:
            key_sub = first_key_sub + key_lane
            s = s + jnp.where((key_sub >= 0) & (key_sub < sub_len), 0.0, NEG_INF).astype(F32)
        m = jnp.max(s, axis=-1, keepdims=True)
        p = jnp.exp2(s - m).astype(BF16)
        pv = _dot(p, jnp.concatenate([v.astype(BF16), ones_cols], axis=1))
        return pv[:, :HEAD_DIM], pv[:, HEAD_DIM:], jnp.broadcast_to(m, (QBLK, HEAD_DIM))

    def keep(pattern, idx, result):
        acc_s[(pattern,) + idx] = result[0]
        l_s[(pattern,) + idx] = result[1]
        m_s[(pattern,) + idx] = result[2]

    def natural_window(p_ref, c_ref, n_ref, row0):
        if row0 == 0:
            return jnp.concatenate([p_ref[HALO - RADIUS:HALO, :], c_ref[0:KBLK - RADIUS, :]], axis=0)
        if row0 == TQ_ATTN - QBLK:
            return jnp.concatenate([c_ref[row0 - RADIUS:TQ_ATTN, :], n_ref[0:RADIUS, :]], axis=0)
        return c_ref[row0 - RADIUS:row0 - RADIUS + KBLK, :]

    def dilation1(g):
        for u in range(BLOCKS_PER_RES4):
            row0 = g * SUB4 + u * QBLK
            res = score_block(q_ref[row0:row0 + QBLK, :], natural_window(kp_ref, kc_ref, kn_ref, row0),
                              natural_window(vp_ref, vc_ref, vn_ref, row0),
                              tile * TQ_ATTN + row0 - RADIUS, seq_len,
                              may_cross_end=row0 in (0, TQ_ATTN - QBLK))
            keep(0, (g, slice(u * QBLK, (u + 1) * QBLK), slice(None)), res)

    def dilation4(r):
        for jb in range(BLOCKS_PER_RES4):
            l0 = jb * QBLK
            keys = slice(HALO4 - RADIUS + l0, HALO4 - RADIUS + l0 + KBLK)
            res = score_block(q4[r, l0:l0 + QBLK, :], k4[r, keys, :], v4[r, keys, :],
                              tile * SUB4 + l0 - RADIUS, seq_len // 4,
                              may_cross_end=jb in (0, BLOCKS_PER_RES4 - 1))
            keep(1, (r, slice(l0, l0 + QBLK), slice(None)), res)

    def dilation16(r):
        for jj in range(N_RES4):
            keys = pl.ds(jj, KBLK, stride=N_RES4)
            queries = pl.ds(jj, QBLK, stride=N_RES4)
            res = score_block(q4[r, queries, :], k4[r, keys, :], v4[r, keys, :],
                              tile * (TQ_ATTN // 16) - RADIUS, seq_len // 16, may_cross_end=True)
            keep(2, (r, queries, slice(None)), res)

    def merge(r):
        for c in range(BLOCKS_PER_RES4):
            perm = (r, slice(c * QBLK, (c + 1) * QBLK), slice(None))
            nat = (c, pl.ds(r, QBLK, stride=N_RES4), slice(None))
            idx = ((0,) + nat, (1,) + perm, (2,) + perm)
            ms = [m_s[i] for i in idx]
            top = jnp.maximum(jnp.maximum(ms[0], ms[1]), ms[2])
            num = jnp.zeros((QBLK, HEAD_DIM), F32)
            den = jnp.zeros((QBLK, HEAD_DIM), F32)
            for i, m in zip(idx, ms):
                e = jnp.exp2(m - top)
                num = num + e * acc_s[i]
                den = den + e * l_s[i]
            o_ref[pl.ds(c * SUB4 + r, QBLK, stride=N_RES4), :] = num / den

    for phase in (dilation1, dilation4, dilation16, merge):
        for r in range(N_RES4):
            phase(r)


def _cast_rows_per_step(n_steps):
    rows = D_MODEL // n_steps if D_MODEL % n_steps == 0 else 0
    return rows if rows % 16 == 0 else 0


def _attn(q, k, v, cast_layer=None, cast_srcs=()):
    s = q.shape[0]
    n_tiles = s // TQ_ATTN
    halo_per_tile = TQ_ATTN // HALO
    n_halo_blocks = s // HALO
    cur = pl.BlockSpec((TQ_ATTN, HEAD_DIM), lambda t, h: (t, h))
    prev = pl.BlockSpec((HALO, HEAD_DIM), lambda t, h: (jnp.maximum(t * halo_per_tile - 1, 0), h))
    nxt = pl.BlockSpec((HALO, HEAD_DIM),
                       lambda t, h: (jnp.minimum((t + 1) * halo_per_tile, n_halo_blocks - 1), h))
    win4 = pltpu.VMEM((N_RES4, SUB4 + 2 * HALO4, HEAD_DIM), F32)
    per_pattern = pltpu.VMEM((len(PATTERN_DILATIONS), N_RES4, SUB4, HEAD_DIM), F32)
    rows = _cast_rows_per_step(n_tiles * N_GROUPS) if cast_srcs else 0
    src_specs = [pl.BlockSpec((None, rows, w.shape[2]), lambda t, h: (cast_layer, t * N_GROUPS + h, 0))
                 for w in cast_srcs]
    dst_specs = [pl.BlockSpec((rows, w.shape[2]), lambda t, h: (t * N_GROUPS + h, 0)) for w in cast_srcs]
    outs = pl.pallas_call(
        functools.partial(_attn_kernel, seq_len=s, n_cast=len(cast_srcs)),
        grid=(n_tiles, N_GROUPS),
        in_specs=[cur, prev, cur, nxt, prev, cur, nxt] + src_specs,
        out_specs=[cur] + dst_specs,
        out_shape=[jax.ShapeDtypeStruct((s, WIDTH), F32)]
                  + [jax.ShapeDtypeStruct(w.shape[1:], BF16) for w in cast_srcs],
        scratch_shapes=[pltpu.VMEM((N_RES4, SUB4, HEAD_DIM), F32), win4, win4,
                        pltpu.VMEM((QBLK, KBLK), F32), per_pattern, per_pattern, per_pattern],
        compiler_params=_params(2),
        name="attn",
    )(q, k, k, k, v, v, v, *cast_srcs)
    return outs[0], outs[1:]


def _out_proj_kernel(ma_ref, bo_ref, gb_ref, w_ref, gpost_ref, x_ref, gffn_ref, x1_ref, h2_ref):
    for r0 in range(0, ma_ref.shape[0], TM_SUB):
        rows = slice(r0, r0 + TM_SUB)
        mb = _rms(bo_ref[rows, :], gb_ref[...]).astype(BF16)
        mix = _dot(jnp.concatenate([ma_ref[rows, :], mb], axis=1), w_ref[...])
        x1 = x_ref[rows, :] + _rms(mix, gpost_ref[...])
        x1_ref[rows, :] = x1
        h2_ref[rows, :] = _rms(x1, gffn_ref[...]).astype(BF16)


def _out_proj(layer, ma, bo, gb, w, gpost, x, gffn):
    s = x.shape[0]
    tm = TM_PROJ
    row = lambda width: pl.BlockSpec((tm, width), lambda i: (i, 0))
    return pl.pallas_call(
        _out_proj_kernel,
        grid=(s // tm,),
        in_specs=[row(WIDTH), row(WIDTH), _resident((1, WIDTH), layer),
                  _whole((2 * WIDTH, D_MODEL)), _resident((1, D_MODEL), layer), row(D_MODEL),
                  _resident((1, D_MODEL), layer)],
        out_specs=[row(D_MODEL), row(D_MODEL)],
        out_shape=[jax.ShapeDtypeStruct((s, D_MODEL), F32), jax.ShapeDtypeStruct((s, D_MODEL), BF16)],
        compiler_params=_params(1),
        name="out_proj",
    )(ma, bo, gb, w, gpost, x, gffn)


def _ffn_kernel(h_ref, wg_ref, wu_ref, wd_ref, x1_ref, gpost_ref, o_ref):
    f = pl.program_id(1)
    last = pl.num_programs(1) - 1

    def partial_sum(rows):
        h = h_ref[rows, :]
        act = (jax.nn.silu(_dot(h, wg_ref[...])) * _dot(h, wu_ref[...])).astype(BF16)
        return _dot(act, wd_ref[...])

    everything = slice(None)

    @pl.when(f == 0)
    def _():
        o_ref[...] = partial_sum(everything)

    @pl.when((f > 0) & (f < last))
    def _():
        o_ref[...] += partial_sum(everything)

    @pl.when(f == last)
    def _():
        half = o_ref.shape[0] // 2
        for r0 in (0, half):
            rows = slice(r0, r0 + half)
            total = o_ref[rows, :] + partial_sum(rows)
            o_ref[rows, :] = x1_ref[rows, :] + _rms(total, gpost_ref[...])


def _ffn(layer, h2, wg, wu, wd, x1, gpost):
    s = x1.shape[0]
    tm, tf = TM_FFN, TF_FFN
    assert D_FF // tf >= 2
    row = pl.BlockSpec((tm, D_MODEL), lambda i, f: (i, 0))
    return pl.pallas_call(
        _ffn_kernel,
        grid=(s // tm, D_FF // tf),
        in_specs=[row, pl.BlockSpec((D_MODEL, tf), lambda i, f: (0, f)),
                  pl.BlockSpec((D_MODEL, tf), lambda i, f: (0, f)),
                  pl.BlockSpec((tf, D_MODEL), lambda i, f: (f, 0)), row,
                  _resident((1, D_MODEL), layer)],
        out_specs=row,
        out_shape=jax.ShapeDtypeStruct((s, D_MODEL), F32),
        compiler_params=_params(2, VMEM_LIMIT_FFN_BYTES),
        name="ffn",
    )(h2, wg, wu, wd, x1, gpost)


def _rope_tables(s):
    half = ROT_DIM // 2
    inv_freq = ROPE_THETA ** (-jnp.arange(half, dtype=F32) / half)
    ang = jnp.arange(s, dtype=F32)[:, None] * inv_freq[None, :]
    cos, sin = jnp.cos(ang), jnp.sin(ang)
    zeros = jnp.zeros_like(sin)
    rest = HEAD_DIM - ROT_DIM
    cos_t = jnp.concatenate([cos, cos, jnp.ones((s, rest), F32)], axis=1)
    sina_t = jnp.concatenate([zeros, sin, jnp.zeros((s, rest), F32)], axis=1)
    sinb_t = jnp.concatenate([-sin, zeros, jnp.zeros((s, rest), F32)], axis=1)
    return cos_t, sina_t, sinb_t


WEIGHT_NAMES = ("w_in", "w_o", "w_gate", "w_up", "w_down")


def _layer(l, x, p, w, tables, cast_srcs=()):
    ma, q, k, v = _in_proj(l, x, p["g_mix_pre"], w["w_in"], p["gmlp_v_gain"], p["gmlp_ws"],
                           p["gmlp_bias"], p["out_gain_a"], *tables)
    bo, cast = _attn(q, k, v, l + 1, cast_srcs)
    x1, h2 = _out_proj(l, ma, bo, p["out_gain_b"], w["w_o"], p["g_mix_post"], x, p["g_ffn_pre"])
    x = _ffn(l, h2, w["w_gate"], w["w_up"], w["w_down"], x1, p["g_ffn_post"])
    return x, cast


def kernel(x_prompt, x_sample, g_mix_pre, w_in, gmlp_v_gain, gmlp_ws, gmlp_bias, out_gain_a, out_gain_b,
           w_o, g_mix_post, g_ffn_pre, w_gate, w_up, w_down, g_ffn_post):
    depth = w_in.shape[0]
    row_vec = lambda t: t[:, None, :]
    p = {
        "g_mix_pre": row_vec(g_mix_pre),
        "gmlp_v_gain": row_vec(gmlp_v_gain),
        "gmlp_ws": gmlp_ws.astype(BF16),
        "gmlp_bias": gmlp_bias[..., None],
        "out_gain_a": row_vec(out_gain_a),
        "out_gain_b": row_vec(out_gain_b),
        "g_mix_post": row_vec(g_mix_post),
        "g_ffn_pre": row_vec(g_ffn_pre),
        "g_ffn_post": row_vec(g_ffn_post),
    }
    shapes = {"w_in": w_in.shape[1:], "w_o": w_o.shape[1:], "w_gate": w_gate.shape[1:],
              "w_up": w_up.shape[1:], "w_down": w_down.shape[1:]}
    f32 = {"w_in": w_in, "w_o": w_o, "w_gate": w_gate, "w_up": w_up,
           "w_down": w_down.reshape(depth, D_MODEL, D_FF)}
    assert x_prompt.shape[0] == 1 and x_sample.shape[0] == 1
    xs = [x.reshape(x.shape[1], x.shape[2]) for x in (x_prompt, x_sample)]
    tables = _rope_tables(max(x.shape[0] for x in xs))
    first = max(range(len(xs)), key=lambda i: xs[i].shape[0])
    in_attention = _cast_rows_per_step(xs[first].shape[0] // TQ_ATTN * N_GROUPS) > 0
    weights = [{n: f32[n][0].astype(BF16).reshape(shapes[n]) for n in WEIGHT_NAMES}]
    x = xs[first]
    for l in range(depth):
        if l + 1 < depth and in_attention:
            x, cast = _layer(l, x, p, weights[l], tables, [f32[n] for n in WEIGHT_NAMES])
            weights.append({n: c.reshape(shapes[n]) for n, c in zip(WEIGHT_NAMES, cast)})
        else:
            x, _ = _layer(l, x, p, weights[l], tables)
            if l + 1 < depth:
                weights.append({n: f32[n][l + 1].astype(BF16).reshape(shapes[n]) for n in WEIGHT_NAMES})
    outs = {first: x}
    for i, x in enumerate(xs):
        if i != first:
            for l in range(depth):
                x, _ = _layer(l, x, p, weights[l], tables)
            outs[i] = x
    return tuple(outs[i].reshape(1, *outs[i].shape) for i in range(len(xs)))
```

```python
import functools

import jax
import jax.numpy as jnp
from jax import lax
from jax.experimental import pallas as pl
from jax.experimental.pallas import tpu as pltpu

D_MODEL = 2048
HEAD_DIM = 128
N_GROUPS = 8
WIDTH = N_GROUPS * HEAD_DIM
PROJ_WIDTH = 5 * WIDTH
CHUNK = 128
PATTERN_DILATIONS = (1, 4, 16)
RADIUS = 64
ROT_DIM = HEAD_DIM // 4
ROPE_THETA = 500000.0
D_FF = 5632
EPS = 1e-6
NEG_INF = -1e30
LOG2E = 1.4426950408889634

BF16 = jnp.bfloat16
F32 = jnp.float32

VMEM_LIMIT_BYTES = 56 * 1024 * 1024
VMEM_LIMIT_FFN_BYTES = 62 * 1024 * 1024

TM_IN = 512
TM_IN_SUB = 512
TM_PROJ = 512
TM_SUB = 128
TM_FFN = 1024
TF_FFN = 512
TQ_ATTN = 2048
HALO = RADIUS * PATTERN_DILATIONS[-1]
QBLK = 128
KBLK = QBLK + 2 * RADIUS
N_RES4 = 4
SUB4 = TQ_ATTN // N_RES4
HALO4 = HALO // N_RES4
BLOCKS_PER_RES4 = SUB4 // QBLK


def _rms(x, gain):
    return x * lax.rsqrt(jnp.mean(x * x, axis=-1, keepdims=True) + EPS) * gain


def _dot(a, b):
    return jnp.dot(a, b, preferred_element_type=F32)


def _resident(shape, layer):
    return pl.BlockSpec((None,) + shape, lambda *_: (layer,) + (0,) * len(shape),
                        pipeline_mode=pl.Buffered(1))


def _whole(shape):
    return pl.BlockSpec(shape, lambda *_: (0,) * len(shape), pipeline_mode=pl.Buffered(1))


def _params(n_grid_axes, vmem_limit_bytes=VMEM_LIMIT_BYTES):
    return pltpu.CompilerParams(
        dimension_semantics=("arbitrary",) * n_grid_axes,
        vmem_limit_bytes=vmem_limit_bytes,
    )


def _spatial_gating(gu_ref, vn_ref, ws_ref, bias_ref, gain_ref, o_ref, row0, n_rows):
    for r0 in range(row0, row0 + n_rows, 2 * CHUNK):
        rows = (slice(r0, r0 + CHUNK), slice(r0 + CHUNK, r0 + 2 * CHUNK))
        gated = ([], [])
        for g in range(N_GROUPS):
            sl = slice(g * HEAD_DIM, (g + 1) * HEAD_DIM)
            vpair = jnp.concatenate([vn_ref[rows[0], sl], vn_ref[rows[1], sl]], axis=1)
            mixed = _dot(ws_ref[g], vpair) + bias_ref[g]
            gated[0].append(gu_ref[rows[0], sl] * mixed[:, :HEAD_DIM])
            gated[1].append(gu_ref[rows[1], sl] * mixed[:, HEAD_DIM:])
        for c in range(2):
            a = jnp.concatenate(gated[c], axis=1)
            o_ref[rows[c], :] = _rms(a, gain_ref[...]).astype(BF16)


def _in_proj_kernel(x_ref, g_ref, w_ref, vgain_ref, ws_ref, bias_ref, again_ref,
                    cos_ref, sina_ref, sinb_ref, ma_ref, q_ref, k_ref, v_ref, gu_s, vn_s):
    for r0 in range(0, x_ref.shape[0], TM_IN_SUB):
        rows = slice(r0, r0 + TM_IN_SUB)
        h = _rms(x_ref[rows, :], g_ref[...]).astype(BF16)

        gu_s[rows, :] = jax.nn.gelu(_dot(h, w_ref[:, 0:WIDTH]))

        va = jax.nn.gelu(_dot(h, w_ref[:, WIDTH:2 * WIDTH]))
        for g in range(N_GROUPS):
            sl = slice(g * HEAD_DIM, (g + 1) * HEAD_DIM)
            vg = va[:, sl]
            mu = jnp.mean(vg, axis=-1, keepdims=True)
            cen = vg - mu
            var = jnp.mean(cen * cen, axis=-1, keepdims=True)
            vn_s[rows, sl] = (cen * lax.rsqrt(var + EPS) * vgain_ref[:, sl]).astype(BF16)

        _spatial_gating(gu_s, vn_s, ws_ref, bias_ref, again_ref, ma_ref, r0, TM_IN_SUB)

        cos = cos_ref[rows, :]
        sina = sina_ref[rows, :]
        sinb = sinb_ref[rows, :]

        def rope_store(dst_ref, col0):
            y = _dot(h, w_ref[:, col0:col0 + WIDTH])
            for hd in range(N_GROUPS):
                sl = slice(hd * HEAD_DIM, (hd + 1) * HEAD_DIM)
                yh = y[:, sl]
                dst_ref[rows, sl] = (yh * cos
                                     + pltpu.roll(yh, ROT_DIM // 2, axis=1) * sina
                                     + pltpu.roll(yh, HEAD_DIM - ROT_DIM // 2, axis=1) * sinb)

        rope_store(q_ref, 2 * WIDTH)
        rope_store(k_ref, 3 * WIDTH)
        v_ref[rows, :] = _dot(h, w_ref[:, 4 * WIDTH:5 * WIDTH])


def _in_proj(layer, x, g, w, vgain, ws, bias, again, cos_t, sina_t, sinb_t):
    s = x.shape[0]
    tm = TM_IN
    row = lambda width: pl.BlockSpec((tm, width), lambda i: (i, 0))
    f32_out = jax.ShapeDtypeStruct((s, WIDTH), F32)
    return pl.pallas_call(
        _in_proj_kernel,
        grid=(s // tm,),
        in_specs=[row(D_MODEL), _resident((1, D_MODEL), layer), _whole((D_MODEL, PROJ_WIDTH)),
                  _resident((1, WIDTH), layer), _resident((N_GROUPS, CHUNK, CHUNK), layer),
                  _resident((N_GROUPS, CHUNK, 1), layer), _resident((1, WIDTH), layer),
                  row(HEAD_DIM), row(HEAD_DIM), row(HEAD_DIM)],
        out_specs=[row(WIDTH)] * 4,
        out_shape=[jax.ShapeDtypeStruct((s, WIDTH), BF16), f32_out, f32_out, f32_out],
        scratch_shapes=[pltpu.VMEM((tm, WIDTH), F32), pltpu.VMEM((tm, WIDTH), BF16)],
        compiler_params=_params(1),
        name="in_proj",
    )(x, g, w, vgain, ws, bias, again, cos_t, sina_t, sinb_t)


def _attn_kernel(*refs, seq_len, cast_blocks):
    q_ref, kp_ref, kc_ref, kn_ref, vp_ref, vc_ref, vn_ref = refs[:7]
    n_cast = len(cast_blocks)
    o_ref = refs[7 + n_cast]
    q4, k4, v4, band, acc_s, m_s, l_s = refs[8 + 2 * n_cast:]
    tile = pl.program_id(0)
    step = tile * pl.num_programs(1) + pl.program_id(1)
    for src, dst, n_blocks in zip(refs[7:7 + n_cast], refs[8 + n_cast:8 + 2 * n_cast], cast_blocks):
        @pl.when(step < n_blocks)
        def _(src=src, dst=dst):
            dst[...] = src[...].astype(BF16)


    for r in range(N_RES4):
        q4[r] = q_ref[pl.ds(r, SUB4, stride=N_RES4), :]
        for win4, p_ref, c_ref, n_ref in ((k4, kp_ref, kc_ref, kn_ref), (v4, vp_ref, vc_ref, vn_ref)):
            win4[r, 0:HALO4, :] = p_ref[pl.ds(r, HALO4, stride=N_RES4), :]
            win4[r, HALO4:HALO4 + SUB4, :] = c_ref[pl.ds(r, SUB4, stride=N_RES4), :]
            win4[r, HALO4 + SUB4:, :] = n_ref[pl.ds(r, HALO4, stride=N_RES4), :]

    qa = lax.broadcasted_iota(jnp.int32, (QBLK, KBLK), 0)
    kb = lax.broadcasted_iota(jnp.int32, (QBLK, KBLK), 1)
    band[...] = jnp.where((kb >= qa) & (kb <= qa + 2 * RADIUS), 0.0, NEG_INF).astype(F32)

    log2_scale = HEAD_DIM ** -0.5 * LOG2E
    ones_cols = jnp.ones((KBLK, HEAD_DIM), BF16)
    key_lane = lax.broadcasted_iota(jnp.int32, (1, KBLK), 1)

    def score_block(q, k, v, first_key_sub, sub_len, may_cross_end):
        s = lax.dot_general(q.astype(BF16), k.astype(BF16), (((1,), (1,)), ((), ())),
                            preferred_element_type=F32) * log2_scale
        s = s + band[...]
        if may_cross_end:
            key_sub = first_key_sub + key_lane
            s = s + jnp.where((key_sub >= 0) & (key_sub < sub_len), 0.0, NEG_INF).astype(F32)
        m = jnp.max(s, axis=-1, keepdims=True)
        p = jnp.exp2(s - m).astype(BF16)
        pv = _dot(p, jnp.concatenate([v.astype(BF16), ones_cols], axis=1))
        return pv[:, :HEAD_DIM], pv[:, HEAD_DIM:], jnp.broadcast_to(m, (QBLK, HEAD_DIM))

    def keep(pattern, idx, result):
        acc_s[(pattern,) + idx] = result[0]
        l_s[(pattern,) + idx] = result[1]
        m_s[(pattern,) + idx] = result[2]

    def natural_window(p_ref, c_ref, n_ref, row0):
        if row0 == 0:
            return jnp.concatenate([p_ref[HALO - RADIUS:HALO, :], c_ref[0:KBLK - RADIUS, :]], axis=0)
        if row0 == TQ_ATTN - QBLK:
            return jnp.concatenate([c_ref[row0 - RADIUS:TQ_ATTN, :], n_ref[0:RADIUS, :]], axis=0)
        return c_ref[row0 - RADIUS:row0 - RADIUS + KBLK, :]

    def dilation1(g):
        for u in range(BLOCKS_PER_RES4):
            row0 = g * SUB4 + u * QBLK
            res = score_block(q_ref[row0:row0 + QBLK, :], natural_window(kp_ref, kc_ref, kn_ref, row0),
                              natural_window(vp_ref, vc_ref, vn_ref, row0),
                              tile * TQ_ATTN + row0 - RADIUS, seq_len,
                              may_cross_end=row0 in (0, TQ_ATTN - QBLK))
            keep(0, (g, slice(u * QBLK, (u + 1) * QBLK), slice(None)), res)

    def dilation4(r):
        for jb in range(BLOCKS_PER_RES4):
            l0 = jb * QBLK
            keys = slice(HALO4 - RADIUS + l0, HALO4 - RADIUS + l0 + KBLK)
            res = score_block(q4[r, l0:l0 + QBLK, :], k4[r, keys, :], v4[r, keys, :],
                              tile * SUB4 + l0 - RADIUS, seq_len // 4,
                              may_cross_end=jb in (0, BLOCKS_PER_RES4 - 1))
            keep(1, (r, slice(l0, l0 + QBLK), slice(None)), res)

    def dilation16(r):
        for jj in range(N_RES4):
            keys = pl.ds(jj, KBLK, stride=N_RES4)
            queries = pl.ds(jj, QBLK, stride=N_RES4)
            res = score_block(q4[r, queries, :], k4[r, keys, :], v4[r, keys, :],
                              tile * (TQ_ATTN // 16) - RADIUS, seq_len // 16, may_cross_end=True)
            keep(2, (r, queries, slice(None)), res)

    def merge(r):
        for c in range(BLOCKS_PER_RES4):
            perm = (r, slice(c * QBLK, (c + 1) * QBLK), slice(None))
            nat = (c, pl.ds(r, QBLK, stride=N_RES4), slice(None))
            idx = ((0,) + nat, (1,) + perm, (2,) + perm)
            ms = [m_s[i] for i in idx]
            top = jnp.maximum(jnp.maximum(ms[0], ms[1]), ms[2])
            num = jnp.zeros((QBLK, HEAD_DIM), F32)
            den = jnp.zeros((QBLK, HEAD_DIM), F32)
            for i, m in zip(idx, ms):
                e = jnp.exp2(m - top)
                num = num + e * acc_s[i]
                den = den + e * l_s[i]
            o_ref[pl.ds(c * SUB4 + r, QBLK, stride=N_RES4), :] = num / den

    for phase in (dilation1, dilation4, dilation16, merge):
        for r in range(N_RES4):
            phase(r)


BF16_SUBLANES = 16


def _cast_slab_rows(n_rows, n_steps):
    for rows in range(BF16_SUBLANES, n_rows + 1, BF16_SUBLANES):
        if n_rows % rows == 0 and n_rows // rows <= n_steps:
            return rows
    return 0


def _attn_steps(s):
    return s // TQ_ATTN * N_GROUPS


def _attn(q, k, v, cast_layer=None, cast_srcs=()):
    s = q.shape[0]
    n_tiles = s // TQ_ATTN
    halo_per_tile = TQ_ATTN // HALO
    n_halo_blocks = s // HALO
    cur = pl.BlockSpec((TQ_ATTN, HEAD_DIM), lambda t, h: (t, h))
    prev = pl.BlockSpec((HALO, HEAD_DIM), lambda t, h: (jnp.maximum(t * halo_per_tile - 1, 0), h))
    nxt = pl.BlockSpec((HALO, HEAD_DIM),
                       lambda t, h: (jnp.minimum((t + 1) * halo_per_tile, n_halo_blocks - 1), h))
    win4 = pltpu.VMEM((N_RES4, SUB4 + 2 * HALO4, HEAD_DIM), F32)
    per_pattern = pltpu.VMEM((len(PATTERN_DILATIONS), N_RES4, SUB4, HEAD_DIM), F32)
    slab_rows = [_cast_slab_rows(w.shape[1], _attn_steps(s)) for w in cast_srcs]
    cast_blocks = tuple(w.shape[1] // rows for w, rows in zip(cast_srcs, slab_rows))

    def slab(n_blocks):
        return lambda t, h: jnp.minimum(t * N_GROUPS + h, n_blocks - 1)

    src_specs = [pl.BlockSpec((None, rows, w.shape[2]), lambda t, h, at=slab(nb): (cast_layer, at(t, h), 0))
                 for w, rows, nb in zip(cast_srcs, slab_rows, cast_blocks)]
    dst_specs = [pl.BlockSpec((rows, w.shape[2]), lambda t, h, at=slab(nb): (at(t, h), 0))
                 for w, rows, nb in zip(cast_srcs, slab_rows, cast_blocks)]
    outs = pl.pallas_call(
        functools.partial(_attn_kernel, seq_len=s, cast_blocks=cast_blocks),
        grid=(n_tiles, N_GROUPS),
        in_specs=[cur, prev, cur, nxt, prev, cur, nxt] + src_specs,
        out_specs=[cur] + dst_specs,
        out_shape=[jax.ShapeDtypeStruct((s, WIDTH), F32)]
                  + [jax.ShapeDtypeStruct(w.shape[1:], BF16) for w in cast_srcs],
        scratch_shapes=[pltpu.VMEM((N_RES4, SUB4, HEAD_DIM), F32), win4, win4,
                        pltpu.VMEM((QBLK, KBLK), F32), per_pattern, per_pattern, per_pattern],
        compiler_params=_params(2),
        name="attn",
    )(q, k, k, k, v, v, v, *cast_srcs)
    return outs[0], outs[1:]


def _out_proj_kernel(ma_ref, bo_ref, gb_ref, w_ref, gpost_ref, x_ref, gffn_ref, x1_ref, h2_ref):
    for r0 in range(0, ma_ref.shape[0], TM_SUB):
        rows = slice(r0, r0 + TM_SUB)
        mb = _rms(bo_ref[rows, :], gb_ref[...]).astype(BF16)
        mix = _dot(jnp.concatenate([ma_ref[rows, :], mb], axis=1), w_ref[...])
        x1 = x_ref[rows, :] + _rms(mix, gpost_ref[...])
        x1_ref[rows, :] = x1
        h2_ref[rows, :] = _rms(x1, gffn_ref[...]).astype(BF16)


def _out_proj(layer, ma, bo, gb, w, gpost, x, gffn):
    s = x.shape[0]
    tm = TM_PROJ
    row = lambda width: pl.BlockSpec((tm, width), lambda i: (i, 0))
    return pl.pallas_call(
        _out_proj_kernel,
        grid=(s // tm,),
        in_specs=[row(WIDTH), row(WIDTH), _resident((1, WIDTH), layer),
                  _whole((2 * WIDTH, D_MODEL)), _resident((1, D_MODEL), layer), row(D_MODEL),
                  _resident((1, D_MODEL), layer)],
        out_specs=[row(D_MODEL), row(D_MODEL)],
        out_shape=[jax.ShapeDtypeStruct((s, D_MODEL), F32), jax.ShapeDtypeStruct((s, D_MODEL), BF16)],
        compiler_params=_params(1),
        name="out_proj",
    )(ma, bo, gb, w, gpost, x, gffn)


def _ffn_kernel(h_ref, wg_ref, wu_ref, wd_ref, x1_ref, gpost_ref, o_ref):
    f = pl.program_id(1)
    last = pl.num_programs(1) - 1

    def partial_sum(rows):
        h = h_ref[rows, :]
        act = (jax.nn.silu(_dot(h, wg_ref[...])) * _dot(h, wu_ref[...])).astype(BF16)
        return _dot(act, wd_ref[...])

    everything = slice(None)

    @pl.when(f == 0)
    def _():
        o_ref[...] = partial_sum(everything)

    @pl.when((f > 0) & (f < last))
    def _():
        o_ref[...] += partial_sum(everything)

    @pl.when(f == last)
    def _():
        half = o_ref.shape[0] // 2
        for r0 in (0, half):
            rows = slice(r0, r0 + half)
            total = o_ref[rows, :] + partial_sum(rows)
            o_ref[rows, :] = x1_ref[rows, :] + _rms(total, gpost_ref[...])


def _ffn(layer, h2, wg, wu, wd, x1, gpost):
    s = x1.shape[0]
    tm, tf = TM_FFN, TF_FFN
    assert D_FF // tf >= 2
    row = pl.BlockSpec((tm, D_MODEL), lambda i, f: (i, 0))
    return pl.pallas_call(
        _ffn_kernel,
        grid=(s // tm, D_FF // tf),
        in_specs=[row, pl.BlockSpec((D_MODEL, tf), lambda i, f: (0, f)),
                  pl.BlockSpec((D_MODEL, tf), lambda i, f: (0, f)),
                  pl.BlockSpec((tf, D_MODEL), lambda i, f: (f, 0)), row,
                  _resident((1, D_MODEL), layer)],
        out_specs=row,
        out_shape=jax.ShapeDtypeStruct((s, D_MODEL), F32),
        compiler_params=_params(2, VMEM_LIMIT_FFN_BYTES),
        name="ffn",
    )(h2, wg, wu, wd, x1, gpost)


def _rope_tables(s):
    half = ROT_DIM // 2
    inv_freq = ROPE_THETA ** (-jnp.arange(half, dtype=F32) / half)
    ang = jnp.arange(s, dtype=F32)[:, None] * inv_freq[None, :]
    cos, sin = jnp.cos(ang), jnp.sin(ang)
    zeros = jnp.zeros_like(sin)
    rest = HEAD_DIM - ROT_DIM
    cos_t = jnp.concatenate([cos, cos, jnp.ones((s, rest), F32)], axis=1)
    sina_t = jnp.concatenate([zeros, sin, jnp.zeros((s, rest), F32)], axis=1)
    sinb_t = jnp.concatenate([-sin, zeros, jnp.zeros((s, rest), F32)], axis=1)
    return cos_t, sina_t, sinb_t


WEIGHT_NAMES = ("w_in", "w_o", "w_gate", "w_up", "w_down")


def _layer(l, x, p, w, tables, cast_srcs=()):
    ma, q, k, v = _in_proj(l, x, p["g_mix_pre"], w["w_in"], p["gmlp_v_gain"], p["gmlp_ws"],
                           p["gmlp_bias"], p["out_gain_a"], *tables)
    bo, cast = _attn(q, k, v, l + 1, cast_srcs)
    x1, h2 = _out_proj(l, ma, bo, p["out_gain_b"], w["w_o"], p["g_mix_post"], x, p["g_ffn_pre"])
    x = _ffn(l, h2, w["w_gate"], w["w_up"], w["w_down"], x1, p["g_ffn_post"])
    return x, cast


def kernel(x_prompt, x_sample, g_mix_pre, w_in, gmlp_v_gain, gmlp_ws, gmlp_bias, out_gain_a, out_gain_b,
           w_o, g_mix_post, g_ffn_pre, w_gate, w_up, w_down, g_ffn_post):
    depth = w_in.shape[0]
    row_vec = lambda t: t[:, None, :]
    p = {
        "g_mix_pre": row_vec(g_mix_pre),
        "gmlp_v_gain": row_vec(gmlp_v_gain),
        "gmlp_ws": gmlp_ws.astype(BF16),
        "gmlp_bias": gmlp_bias[..., None],
        "out_gain_a": row_vec(out_gain_a),
        "out_gain_b": row_vec(out_gain_b),
        "g_mix_post": row_vec(g_mix_post),
        "g_ffn_pre": row_vec(g_ffn_pre),
        "g_ffn_post": row_vec(g_ffn_post),
    }
    f32 = {"w_in": w_in, "w_o": w_o, "w_gate": w_gate, "w_up": w_up, "w_down": w_down}
    assert x_prompt.shape[0] == 1 and x_sample.shape[0] == 1
    xs = [x.reshape(x.shape[1], x.shape[2]) for x in (x_prompt, x_sample)]
    tables = _rope_tables(max(x.shape[0] for x in xs))
    first = max(range(len(xs)), key=lambda i: xs[i].shape[0])
    in_attention = all(_cast_slab_rows(f32[n].shape[1], _attn_steps(xs[first].shape[0])) > 0
                       for n in WEIGHT_NAMES)
    weights = [{n: f32[n][0].astype(BF16) for n in WEIGHT_NAMES}]
    x = xs[first]
    for l in range(depth):
        if l + 1 < depth and in_attention:
            x, cast = _layer(l, x, p, weights[l], tables, [f32[n] for n in WEIGHT_NAMES])
            weights.append(dict(zip(WEIGHT_NAMES, cast)))
        else:
            x, _ = _layer(l, x, p, weights[l], tables)
            if l + 1 < depth:
                weights.append({n: f32[n][l + 1].astype(BF16) for n in WEIGHT_NAMES})
    outs = {first: x}
    for i, x in enumerate(xs):
        if i != first:
            for l in range(depth):
                x, _ = _layer(l, x, p, weights[l], tables)
            outs[i] = x
    return tuple(outs[i].reshape(1, *outs[i].shape) for i in range(len(xs)))
```

```python
import functools

import jax
import jax.numpy as jnp
from jax import lax
from jax.experimental import pallas as pl
from jax.experimental.pallas import tpu as pltpu

D_MODEL = 2048
HEAD_DIM = 128
N_GROUPS = 8
WIDTH = N_GROUPS * HEAD_DIM
PROJ_WIDTH = 5 * WIDTH
CHUNK = 128
PATTERN_DILATIONS = (1, 4, 16)
RADIUS = 64
ROT_DIM = HEAD_DIM // 4
ROPE_THETA = 500000.0
D_FF = 5632
EPS = 1e-6
NEG_INF = -1e30
LOG2E = 1.4426950408889634

BF16 = jnp.bfloat16
F32 = jnp.float32

VMEM_LIMIT_BYTES = 56 * 1024 * 1024
VMEM_LIMIT_FFN_BYTES = 62 * 1024 * 1024

TM_IN = 512
TM_IN_SUB = 512
TM_PROJ = 512
TM_SUB = 128
TM_FFN = 1024
TF_FFN = 512
TQ_ATTN = 2048
HALO = RADIUS * PATTERN_DILATIONS[-1]
QBLK = 128
KBLK = QBLK + 2 * RADIUS
N_RES4 = 4
SUB4 = TQ_ATTN // N_RES4
HALO4 = HALO // N_RES4
BLOCKS_PER_RES4 = SUB4 // QBLK


def _rms(x, gain):
    return x * lax.rsqrt(jnp.mean(x * x, axis=-1, keepdims=True) + EPS) * gain


def _dot(a, b):
    return jnp.dot(a, b, preferred_element_type=F32)


def _resident(shape, layer):
    return pl.BlockSpec((None,) + shape, lambda *_: (layer,) + (0,) * len(shape),
                        pipeline_mode=pl.Buffered(1))


def _whole(shape):
    return pl.BlockSpec(shape, lambda *_: (0,) * len(shape), pipeline_mode=pl.Buffered(1))


def _params(n_grid_axes, vmem_limit_bytes=VMEM_LIMIT_BYTES):
    return pltpu.CompilerParams(
        dimension_semantics=("arbitrary",) * n_grid_axes,
        vmem_limit_bytes=vmem_limit_bytes,
    )


def _spatial_gating(gu_ref, vn_ref, ws_ref, bias_ref, gain_ref, o_ref, row0, n_rows):
    for r0 in range(row0, row0 + n_rows, 2 * CHUNK):
        rows = (slice(r0, r0 + CHUNK), slice(r0 + CHUNK, r0 + 2 * CHUNK))
        gated = ([], [])
        for g in range(N_GROUPS):
            sl = slice(g * HEAD_DIM, (g + 1) * HEAD_DIM)
            vpair = jnp.concatenate([vn_ref[rows[0], sl], vn_ref[rows[1], sl]], axis=1)
            mixed = _dot(ws_ref[g], vpair) + bias_ref[g]
            gated[0].append(gu_ref[rows[0], sl] * mixed[:, :HEAD_DIM])
            gated[1].append(gu_ref[rows[1], sl] * mixed[:, HEAD_DIM:])
        for c in range(2):
            a = jnp.concatenate(gated[c], axis=1)
            o_ref[rows[c], :] = _rms(a, gain_ref[...]).astype(BF16)


def _in_proj_kernel(x_ref, g_ref, w_ref, vgain_ref, ws_ref, bias_ref, again_ref,
                    cos_ref, sina_ref, sinb_ref, ma_ref, q_ref, k_ref, v_ref, gu_s, vn_s):
    for r0 in range(0, x_ref.shape[0], TM_IN_SUB):
        rows = slice(r0, r0 + TM_IN_SUB)
        h = _rms(x_ref[rows, :], g_ref[...]).astype(BF16)

        gu_s[rows, :] = jax.nn.gelu(_dot(h, w_ref[:, 0:WIDTH]))

        va = jax.nn.gelu(_dot(h, w_ref[:, WIDTH:2 * WIDTH]))
        for g in range(N_GROUPS):
            sl = slice(g * HEAD_DIM, (g + 1) * HEAD_DIM)
            vg = va[:, sl]
            mu = jnp.mean(vg, axis=-1, keepdims=True)
            cen = vg - mu
            var = jnp.mean(cen * cen, axis=-1, keepdims=True)
            vn_s[rows, sl] = (cen * lax.rsqrt(var + EPS) * vgain_ref[:, sl]).astype(BF16)

        _spatial_gating(gu_s, vn_s, ws_ref, bias_ref, again_ref, ma_ref, r0, TM_IN_SUB)

        cos = cos_ref[rows, :]
        sina = sina_ref[rows, :]
        sinb = sinb_ref[rows, :]

        def rope_store(dst_ref, col0):
            y = _dot(h, w_ref[:, col0:col0 + WIDTH])
            for hd in range(N_GROUPS):
                sl = slice(hd * HEAD_DIM, (hd + 1) * HEAD_DIM)
                yh = y[:, sl]
                dst_ref[rows, sl] = (yh * cos
                                     + pltpu.roll(yh, ROT_DIM // 2, axis=1) * sina
                                     + pltpu.roll(yh, HEAD_DIM - ROT_DIM // 2, axis=1) * sinb)

        rope_store(q_ref, 2 * WIDTH)
        rope_store(k_ref, 3 * WIDTH)
        v_ref[rows, :] = _dot(h, w_ref[:, 4 * WIDTH:5 * WIDTH])


def _in_proj(layer, x, g, w, vgain, ws, bias, again, cos_t, sina_t, sinb_t):
    s = x.shape[0]
    tm = TM_IN
    row = lambda width: pl.BlockSpec((tm, width), lambda i: (i, 0))
    f32_out = jax.ShapeDtypeStruct((s, WIDTH), F32)
    return pl.pallas_call(
        _in_proj_kernel,
        grid=(s // tm,),
        in_specs=[row(D_MODEL), _resident((1, D_MODEL), layer), _whole((D_MODEL, PROJ_WIDTH)),
                  _resident((1, WIDTH), layer), _resident((N_GROUPS, CHUNK, CHUNK), layer),
                  _resident((N_GROUPS, CHUNK, 1), layer), _resident((1, WIDTH), layer),
                  row(HEAD_DIM), row(HEAD_DIM), row(HEAD_DIM)],
        out_specs=[row(WIDTH)] * 4,
        out_shape=[jax.ShapeDtypeStruct((s, WIDTH), BF16), f32_out, f32_out, f32_out],
        scratch_shapes=[pltpu.VMEM((tm, WIDTH), F32), pltpu.VMEM((tm, WIDTH), BF16)],
        compiler_params=_params(1),
        name="in_proj",
    )(x, g, w, vgain, ws, bias, again, cos_t, sina_t, sinb_t)


def _attn_kernel(*refs, seq_len, cast_blocks):
    q_ref, kp_ref, kc_ref, kn_ref, vp_ref, vc_ref, vn_ref = refs[:7]
    n_cast = len(cast_blocks)
    o_ref = refs[7 + n_cast]
    q4, k4, v4, band, acc_s, m_s, l_s = refs[8 + 2 * n_cast:]
    tile = pl.program_id(0)
    step = tile * pl.num_programs(1) + pl.program_id(1)
    for src, dst, n_blocks in zip(refs[7:7 + n_cast], refs[8 + n_cast:8 + 2 * n_cast], cast_blocks):
        @pl.when(step < n_blocks)
        def _(src=src, dst=dst):
            dst[...] = src[...].astype(BF16)


    for r in range(N_RES4):
        q4[r] = q_ref[pl.ds(r, SUB4, stride=N_RES4), :]
        for win4, p_ref, c_ref, n_ref in ((k4, kp_ref, kc_ref, kn_ref), (v4, vp_ref, vc_ref, vn_ref)):
            win4[r, 0:HALO4, :] = p_ref[pl.ds(r, HALO4, stride=N_RES4), :]
            win4[r, HALO4:HALO4 + SUB4, :] = c_ref[pl.ds(r, SUB4, stride=N_RES4), :]
            win4[r, HALO4 + SUB4:, :] = n_ref[pl.ds(r, HALO4, stride=N_RES4), :]

    qa = lax.broadcasted_iota(jnp.int32, (QBLK, KBLK), 0)
    kb = lax.broadcasted_iota(jnp.int32, (QBLK, KBLK), 1)
    band[...] = jnp.where((kb >= qa) & (kb <= qa + 2 * RADIUS), 0.0, NEG_INF).astype(F32)

    log2_scale = HEAD_DIM ** -0.5 * LOG2E
    ones_cols = jnp.ones((KBLK, HEAD_DIM), BF16)
    key_lane = lax.broadcasted_iota(jnp.int32, (1, KBLK), 1)

    def score_block(q, k, v, first_key_sub, sub_len, may_cross_end):
        s = lax.dot_general(q.astype(BF16), k.astype(BF16), (((1,), (1,)), ((), ())),
                            preferred_element_type=F32) * log2_scale
        s = s + band[...]
        if may_cross_end:
            key_sub = first_key_sub + key_lane
            s = s + jnp.where((key_sub >= 0) & (key_sub < sub_len), 0.0, NEG_INF).astype(F32)
        m = jnp.max(s, axis=-1, keepdims=True)
        p = jnp.exp2(s - m).astype(BF16)
        pv = _dot(p, jnp.concatenate([v.astype(BF16), ones_cols], axis=1))
        return pv[:, :HEAD_DIM], pv[:, HEAD_DIM:], jnp.broadcast_to(m, (QBLK, HEAD_DIM))

    def keep(pattern, idx, result):
        acc_s[(pattern,) + idx] = result[0]
        l_s[(pattern,) + idx] = result[1]
        m_s[(pattern,) + idx] = result[2]

    def natural_window(p_ref, c_ref, n_ref, row0):
        if row0 == 0:
            return jnp.concatenate([p_ref[HALO - RADIUS:HALO, :], c_ref[0:KBLK - RADIUS, :]], axis=0)
        if row0 == TQ_ATTN - QBLK:
            return jnp.concatenate([c_ref[row0 - RADIUS:TQ_ATTN, :], n_ref[0:RADIUS, :]], axis=0)
        return c_ref[row0 - RADIUS:row0 - RADIUS + KBLK, :]

    def dilation1(g):
        for u in range(BLOCKS_PER_RES4):
            row0 = g * SUB4 + u * QBLK
            res = score_block(q_ref[row0:row0 + QBLK, :], natural_window(kp_ref, kc_ref, kn_ref, row0),
                              natural_window(vp_ref, vc_ref, vn_ref, row0),
                              tile * TQ_ATTN + row0 - RADIUS, seq_len,
                              may_cross_end=row0 in (0, TQ_ATTN - QBLK))
            keep(0, (g, slice(u * QBLK, (u + 1) * QBLK), slice(None)), res)

    def dilation4(r):
        for jb in range(BLOCKS_PER_RES4):
            l0 = jb * QBLK
            keys = slice(HALO4 - RADIUS + l0, HALO4 - RADIUS + l0 + KBLK)
            res = score_block(q4[r, l0:l0 + QBLK, :], k4[r, keys, :], v4[r, keys, :],
                              tile * SUB4 + l0 - RADIUS, seq_len // 4,
                              may_cross_end=jb in (0, BLOCKS_PER_RES4 - 1))
            keep(1, (r, slice(l0, l0 + QBLK), slice(None)), res)

    def dilation16(r):
        for jj in range(N_RES4):
            keys = pl.ds(jj, KBLK, stride=N_RES4)
            queries = pl.ds(jj, QBLK, stride=N_RES4)
            res = score_block(q4[r, queries, :], k4[r, keys, :], v4[r, keys, :],
                              tile * (TQ_ATTN // 16) - RADIUS, seq_len // 16, may_cross_end=True)
            keep(2, (r, queries, slice(None)), res)

    def merge(r):
        for c in range(BLOCKS_PER_RES4):
            perm = (r, slice(c * QBLK, (c + 1) * QBLK), slice(None))
            nat = (c, pl.ds(r, QBLK, stride=N_RES4), slice(None))
            idx = ((0,) + nat, (1,) + perm, (2,) + perm)
            ms = [m_s[i] for i in idx]
            top = jnp.maximum(jnp.maximum(ms[0], ms[1]), ms[2])
            num = jnp.zeros((QBLK, HEAD_DIM), F32)
            den = jnp.zeros((QBLK, HEAD_DIM), F32)
            for i, m in zip(idx, ms):
                e = jnp.exp2(m - top)
                num = num + e * acc_s[i]
                den = den + e * l_s[i]
            o_ref[pl.ds(c * SUB4 + r, QBLK, stride=N_RES4), :] = num / den

    for phase in (dilation1, dilation4, dilation16, merge):
        for r in range(N_RES4):
            phase(r)


BF16_SUBLANES = 16


def _cast_slab_rows(n_rows, n_steps):
    for rows in range(BF16_SUBLANES, n_rows + 1, BF16_SUBLANES):
        if n_rows % rows == 0 and n_rows // rows <= n_steps:
            return rows
    return 0


def _attn_steps(s):
    return s // TQ_ATTN * N_GROUPS


def _attn(q, k, v, cast_plan=()):
    cast_srcs = [w for w, _ in cast_plan]
    s = q.shape[0]
    n_tiles = s // TQ_ATTN
    halo_per_tile = TQ_ATTN // HALO
    n_halo_blocks = s // HALO
    cur = pl.BlockSpec((TQ_ATTN, HEAD_DIM), lambda t, h: (t, h))
    prev = pl.BlockSpec((HALO, HEAD_DIM), lambda t, h: (jnp.maximum(t * halo_per_tile - 1, 0), h))
    nxt = pl.BlockSpec((HALO, HEAD_DIM),
                       lambda t, h: (jnp.minimum((t + 1) * halo_per_tile, n_halo_blocks - 1), h))
    win4 = pltpu.VMEM((N_RES4, SUB4 + 2 * HALO4, HEAD_DIM), F32)
    per_pattern = pltpu.VMEM((len(PATTERN_DILATIONS), N_RES4, SUB4, HEAD_DIM), F32)
    slab_rows = [_cast_slab_rows(w.shape[1], _attn_steps(s)) for w in cast_srcs]
    cast_blocks = tuple(w.shape[1] // rows for w, rows in zip(cast_srcs, slab_rows))

    def slab(n_blocks):
        return lambda t, h: jnp.minimum(t * N_GROUPS + h, n_blocks - 1)

    src_specs = [pl.BlockSpec((None, rows, w.shape[2]), lambda t, h, at=slab(nb), ly=ly: (ly, at(t, h), 0))
                 for (w, ly), rows, nb in zip(cast_plan, slab_rows, cast_blocks)]
    dst_specs = [pl.BlockSpec((rows, w.shape[2]), lambda t, h, at=slab(nb): (at(t, h), 0))
                 for w, rows, nb in zip(cast_srcs, slab_rows, cast_blocks)]
    outs = pl.pallas_call(
        functools.partial(_attn_kernel, seq_len=s, cast_blocks=cast_blocks),
        grid=(n_tiles, N_GROUPS),
        in_specs=[cur, prev, cur, nxt, prev, cur, nxt] + src_specs,
        out_specs=[cur] + dst_specs,
        out_shape=[jax.ShapeDtypeStruct((s, WIDTH), F32)]
                  + [jax.ShapeDtypeStruct(w.shape[1:], BF16) for w in cast_srcs],
        scratch_shapes=[pltpu.VMEM((N_RES4, SUB4, HEAD_DIM), F32), win4, win4,
                        pltpu.VMEM((QBLK, KBLK), F32), per_pattern, per_pattern, per_pattern],
        compiler_params=_params(2),
        name="attn",
    )(q, k, k, k, v, v, v, *cast_srcs)
    return outs[0], outs[1:]


def _out_proj_kernel(ma_ref, bo_ref, gb_ref, w_ref, gpost_ref, x_ref, gffn_ref, x1_ref, h2_ref):
    for r0 in range(0, ma_ref.shape[0], TM_SUB):
        rows = slice(r0, r0 + TM_SUB)
        mb = _rms(bo_ref[rows, :], gb_ref[...]).astype(BF16)
        mix = _dot(jnp.concatenate([ma_ref[rows, :], mb], axis=1), w_ref[...])
        x1 = x_ref[rows, :] + _rms(mix, gpost_ref[...])
        x1_ref[rows, :] = x1
        h2_ref[rows, :] = _rms(x1, gffn_ref[...]).astype(BF16)


def _out_proj(layer, ma, bo, gb, w, gpost, x, gffn):
    s = x.shape[0]
    tm = TM_PROJ
    row = lambda width: pl.BlockSpec((tm, width), lambda i: (i, 0))
    return pl.pallas_call(
        _out_proj_kernel,
        grid=(s // tm,),
        in_specs=[row(WIDTH), row(WIDTH), _resident((1, WIDTH), layer),
                  _whole((2 * WIDTH, D_MODEL)), _resident((1, D_MODEL), layer), row(D_MODEL),
                  _resident((1, D_MODEL), layer)],
        out_specs=[row(D_MODEL), row(D_MODEL)],
        out_shape=[jax.ShapeDtypeStruct((s, D_MODEL), F32), jax.ShapeDtypeStruct((s, D_MODEL), BF16)],
        compiler_params=_params(1),
        name="out_proj",
    )(ma, bo, gb, w, gpost, x, gffn)


def _ffn_kernel(h_ref, wg_ref, wu_ref, wd_ref, x1_ref, gpost_ref, o_ref):
    f = pl.program_id(1)
    last = pl.num_programs(1) - 1

    def partial_sum(rows):
        h = h_ref[rows, :]
        act = (jax.nn.silu(_dot(h, wg_ref[...])) * _dot(h, wu_ref[...])).astype(BF16)
        return _dot(act, wd_ref[...])

    everything = slice(None)

    @pl.when(f == 0)
    def _():
        o_ref[...] = partial_sum(everything)

    @pl.when((f > 0) & (f < last))
    def _():
        o_ref[...] += partial_sum(everything)

    @pl.when(f == last)
    def _():
        half = o_ref.shape[0] // 2
        for r0 in (0, half):
            rows = slice(r0, r0 + half)
            total = o_ref[rows, :] + partial_sum(rows)
            o_ref[rows, :] = x1_ref[rows, :] + _rms(total, gpost_ref[...])


def _ffn(layer, h2, wg, wu, wd, x1, gpost):
    s = x1.shape[0]
    tm, tf = TM_FFN, TF_FFN
    assert D_FF // tf >= 2
    row = pl.BlockSpec((tm, D_MODEL), lambda i, f: (i, 0))
    return pl.pallas_call(
        _ffn_kernel,
        grid=(s // tm, D_FF // tf),
        in_specs=[row, pl.BlockSpec((D_MODEL, tf), lambda i, f: (0, f)),
                  pl.BlockSpec((D_MODEL, tf), lambda i, f: (0, f)),
                  pl.BlockSpec((tf, D_MODEL), lambda i, f: (f, 0)), row,
                  _resident((1, D_MODEL), layer)],
        out_specs=row,
        out_shape=jax.ShapeDtypeStruct((s, D_MODEL), F32),
        compiler_params=_params(2, VMEM_LIMIT_FFN_BYTES),
        name="ffn",
    )(h2, wg, wu, wd, x1, gpost)


def _rope_tables(s):
    half = ROT_DIM // 2
    inv_freq = ROPE_THETA ** (-jnp.arange(half, dtype=F32) / half)
    ang = jnp.arange(s, dtype=F32)[:, None] * inv_freq[None, :]
    cos, sin = jnp.cos(ang), jnp.sin(ang)
    zeros = jnp.zeros_like(sin)
    rest = HEAD_DIM - ROT_DIM
    cos_t = jnp.concatenate([cos, cos, jnp.ones((s, rest), F32)], axis=1)
    sina_t = jnp.concatenate([zeros, sin, jnp.zeros((s, rest), F32)], axis=1)
    sinb_t = jnp.concatenate([-sin, zeros, jnp.zeros((s, rest), F32)], axis=1)
    return cos_t, sina_t, sinb_t


WEIGHT_NAMES = ("w_in", "w_o", "w_gate", "w_up", "w_down")


def _layer(l, x, p, w, tables, f32=None, cast_names=()):
    ma, q, k, v = _in_proj(l, x, p["g_mix_pre"], w["w_in"], p["gmlp_v_gain"], p["gmlp_ws"],
                           p["gmlp_bias"], p["out_gain_a"], *tables)
    bo, cast = _attn(q, k, v, [(f32[n], ly) for n, ly in cast_names])
    w = {**w, **{n: c for (n, ly), c in zip(cast_names, cast) if ly == l}}
    x1, h2 = _out_proj(l, ma, bo, p["out_gain_b"], w["w_o"], p["g_mix_post"], x, p["g_ffn_pre"])
    x = _ffn(l, h2, w["w_gate"], w["w_up"], w["w_down"], x1, p["g_ffn_post"])
    return x, w, {n: c for (n, ly), c in zip(cast_names, cast) if ly == l + 1}


def kernel(x_prompt, x_sample, g_mix_pre, w_in, gmlp_v_gain, gmlp_ws, gmlp_bias, out_gain_a, out_gain_b,
           w_o, g_mix_post, g_ffn_pre, w_gate, w_up, w_down, g_ffn_post):
    depth = w_in.shape[0]
    row_vec = lambda t: t[:, None, :]
    p = {
        "g_mix_pre": row_vec(g_mix_pre),
        "gmlp_v_gain": row_vec(gmlp_v_gain),
        "gmlp_ws": gmlp_ws.astype(BF16),
        "gmlp_bias": gmlp_bias[..., None],
        "out_gain_a": row_vec(out_gain_a),
        "out_gain_b": row_vec(out_gain_b),
        "g_mix_post": row_vec(g_mix_post),
        "g_ffn_pre": row_vec(g_ffn_pre),
        "g_ffn_post": row_vec(g_ffn_post),
    }
    f32 = {"w_in": w_in, "w_o": w_o, "w_gate": w_gate, "w_up": w_up, "w_down": w_down}
    assert x_prompt.shape[0] == 1 and x_sample.shape[0] == 1
    xs = [x.reshape(x.shape[1], x.shape[2]) for x in (x_prompt, x_sample)]
    tables = _rope_tables(max(x.shape[0] for x in xs))
    first = max(range(len(xs)), key=lambda i: xs[i].shape[0])
    in_attention = all(_cast_slab_rows(f32[n].shape[1], _attn_steps(xs[first].shape[0])) > 0
                       for n in WEIGHT_NAMES)
    if in_attention:
        weights = [{"w_in": w_in[0].astype(BF16)}]
    else:
        weights = [{n: f32[n][l].astype(BF16) for n in WEIGHT_NAMES} for l in range(depth)]
    x = xs[first]
    for l in range(depth):
        names = []
        if in_attention:
            names = [(n, l) for n in WEIGHT_NAMES if n not in weights[l]]
            names += [(n, l + 1) for n in WEIGHT_NAMES] if l + 1 < depth else []
        x, weights[l], following = _layer(l, x, p, weights[l], tables, f32, names)
        if in_attention and l + 1 < depth:
            weights.append(following)
    outs = {first: x}
    for i, x in enumerate(xs):
        if i != first:
            for l in range(depth):
                x, _, _ = _layer(l, x, p, weights[l], tables)
            outs[i] = x
    return tuple(outs[i].reshape(1, *outs[i].shape) for i in range(len(xs)))
```
